```python
import jax
import jax.numpy as jnp
from jax import lax
import numpy as np

D_MODEL = 2048
BATCH = 2
SEQ = 16384
DEPTH = 2
DEC_BATCH = 16
DEC_SEQ = 16
PAST_LEN = 4096

CHUNK = 64
N_BRANCH = 4
D_A = D_MODEL // 4
D_B = D_MODEL // 4
D_C = D_MODEL // 4
D_D = D_MODEL // 4
CONV_A_WIDTH = 31
CONV_B_WIDTH = 3
SPATIAL_CHUNK = 128
C_GROUPS = 4
C_GROUP_W = D_C // C_GROUPS
POOL_WINDOWS = (2, 4, 8, 16)
POOL_GROUP_W = D_D // len(POOL_WINDOWS)
POOL_HIST = max(POOL_WINDOWS) - 1
D_FF = -(-8 * D_MODEL // (3 * 256)) * 256

COL_A = 2 * D_A
COL_B = 3 * D_B
COL_C = 2 * D_C
COL_D = D_D
COL_G = N_BRANCH * D_MODEL
IN_COLS = COL_A + COL_B + COL_C + COL_D + COL_G
SPLIT_IDX = (COL_A, COL_A + COL_B, COL_A + COL_B + COL_C, COL_A + COL_B + COL_C + COL_D)

RMS_EPS = 1e-6
LN_EPS = 1e-5

kernel_name = "gated_parallel_conv_pool_gmlp_stream_step"


def rmsnorm(x, g):
    xf = x.astype(jnp.float32)
    y = xf * lax.rsqrt(jnp.mean(xf * xf, axis=-1, keepdims=True) + RMS_EPS)
    return (y * g.astype(jnp.float32)).astype(x.dtype)


def layernorm(x, g, b):
    xf = x.astype(jnp.float32)
    mu = jnp.mean(xf, axis=-1, keepdims=True)
    var = jnp.mean(jnp.square(xf - mu), axis=-1, keepdims=True)
    y = (xf - mu) * lax.rsqrt(var + LN_EPS)
    return (y * g.astype(jnp.float32) + b.astype(jnp.float32)).astype(x.dtype)


def causal_dwconv(x, hist, w):
    k = w.shape[0]
    xp = jnp.concatenate([hist.astype(x.dtype), x], axis=1)
    y = lax.conv_general_dilated(xp, w[:, None, :].astype(x.dtype), window_strides=(1,), padding='VALID',
                                 dimension_numbers=('NWC', 'WIO', 'NWC'), feature_group_count=x.shape[-1])
    return y, xp[:, xp.shape[1] - (k - 1):]


def multiscale_pool(x, hist, start):
    t = x.shape[1]
    xp = jnp.concatenate([hist.astype(x.dtype), x], axis=1)
    cs = jnp.pad(jnp.cumsum(xp.astype(jnp.float32), axis=1), ((0, 0), (1, 0), (0, 0)))
    n_avail = start + jnp.arange(t, dtype=jnp.int32) + 1
    parts = []
    for g, w in enumerate(POOL_WINDOWS):
        lo, hi = g * POOL_GROUP_W, (g + 1) * POOL_GROUP_W
        s_end = cs[:, POOL_HIST + 1:POOL_HIST + 1 + t, lo:hi]
        s_beg = cs[:, POOL_HIST + 1 - w:POOL_HIST + 1 - w + t, lo:hi]
        cnt = jnp.minimum(n_avail, w).astype(jnp.float32)[None, :, None]
        parts.append((s_end - s_beg) / cnt)
    pooled = jnp.concatenate(parts, axis=-1) - x.astype(jnp.float32)
    return pooled.astype(x.dtype), xp[:, xp.shape[1] - POOL_HIST:]


def spatial_gating(u, v, w_s, b_s):
    nb, t, _ = v.shape
    n = -(-t // SPATIAL_CHUNK)
    vp = jnp.pad(v, ((0, 0), (0, n * SPATIAL_CHUNK - t), (0, 0)))
    vp = vp.reshape(nb, n, SPATIAL_CHUNK, C_GROUPS, C_GROUP_W)
    mask = jnp.tril(jnp.ones((SPATIAL_CHUNK, SPATIAL_CHUNK), dtype=bool))
    wm = jnp.where(mask[None], w_s, jnp.zeros((), w_s.dtype))
    mixed = jnp.einsum('gts,bnsgc->bntgc', wm, vp) + b_s.T[None, None, :, :, None]
    mixed = mixed.reshape(nb, n * SPATIAL_CHUNK, D_C)[:, :t]
    return u * mixed


def trunk_layer(x, hist_a, hist_b, hist_d, start, norm_mix_g, w_in, conv_a_w, conv_a_b, ln_a_g, ln_a_b,
                w_out_a, conv_b_w, w_out_b, ln_c_g, ln_c_b, spatial_w, spatial_b, w_out_c, pool_w,
                pool_scale, w_out_d, w_o, norm_ffn_g, ffn_w1, ffn_w3, ffn_w2):
    nb, t, _ = x.shape
    h = rmsnorm(x, norm_mix_g)
    z = h @ w_in
    za, zb, zc, zd, zg = jnp.split(z, SPLIT_IDX, axis=-1)

    a = za[..., :D_A] * jax.nn.sigmoid(za[..., D_A:])
    a_conv, new_a = causal_dwconv(a, hist_a, conv_a_w)
    a_out = jax.nn.silu(layernorm(a_conv + conv_a_b, ln_a_g, ln_a_b)) @ w_out_a

    bg, cg, hb = jnp.split(zb, 3, axis=-1)
    b_conv, new_b = causal_dwconv(cg * hb, hist_b, conv_b_w)
    b_out = (bg * b_conv) @ w_out_b

    zc = jax.nn.gelu(zc, approximate=False)
    u, v = jnp.split(zc, 2, axis=-1)
    v = layernorm(v, ln_c_g, ln_c_b)
    c_out = spatial_gating(u, v, spatial_w, spatial_b) @ w_out_c
    v_rows = v[:, ((t - 1) // SPATIAL_CHUNK) * SPATIAL_CHUNK:]

    d_pool, new_d = multiscale_pool(zd, hist_d, start)
    d_mix = jnp.einsum('btgc,gcd->btgd', d_pool.reshape(nb, t, len(POOL_WINDOWS), POOL_GROUP_W), pool_w)
    d_out = (d_mix.reshape(nb, t, D_D) * pool_scale) @ w_out_d

    gate = jax.nn.sigmoid(zg)
    merged = (gate[..., 0 * D_MODEL:1 * D_MODEL] * a_out + gate[..., 1 * D_MODEL:2 * D_MODEL] * b_out
              + gate[..., 2 * D_MODEL:3 * D_MODEL] * c_out + gate[..., 3 * D_MODEL:4 * D_MODEL] * d_out)
    x = x + merged @ w_o

    h2 = rmsnorm(x, norm_ffn_g)
    x = x + (jax.nn.silu(h2 @ ffn_w1) * (h2 @ ffn_w3)) @ ffn_w2
    return x, new_a, new_b, new_d, v_rows


def setup_inputs(seed: int = 0) -> dict:
    key = jax.random.key(seed)
    ks = iter(jax.random.split(key, 32))

    def nrm(shape, scale):
        return jax.random.normal(next(ks), shape, jnp.float32) * scale

    L = DEPTH
    return {
        'x_prompt': nrm((BATCH, SEQ, D_MODEL), 1.0),
        'x_sample': nrm((DEC_BATCH, DEC_SEQ, D_MODEL), 1.0),
        'state_conv_a': nrm((L, DEC_BATCH, CONV_A_WIDTH - 1, D_A), 0.5),
        'state_conv_b': nrm((L, DEC_BATCH, CONV_B_WIDTH - 1, D_B), 0.5),
        'state_pool': nrm((L, DEC_BATCH, POOL_HIST, D_D), 1.0),
        'norm_mix_g': 1.0 + nrm((L, D_MODEL), 0.05),
        'w_in': nrm((L, D_MODEL, IN_COLS), D_MODEL ** -0.5),
        'conv_a_w': nrm((L, CONV_A_WIDTH, D_A), CONV_A_WIDTH ** -0.5),
        'conv_a_b': nrm((L, D_A), 0.02),
        'ln_a_g': 1.0 + nrm((L, D_A), 0.05),
        'ln_a_b': nrm((L, D_A), 0.02),
        'w_out_a': nrm((L, D_A, D_MODEL), D_A ** -0.5),
        'conv_b_w': nrm((L, CONV_B_WIDTH, D_B), CONV_B_WIDTH ** -0.5),
        'w_out_b': nrm((L, D_B, D_MODEL), D_B ** -0.5),
        'ln_c_g': 1.0 + nrm((L, D_C), 0.05),
        'ln_c_b': nrm((L, D_C), 0.02),
        'spatial_w': nrm((L, C_GROUPS, SPATIAL_CHUNK, SPATIAL_CHUNK), SPATIAL_CHUNK ** -0.5),
        'spatial_b': 1.0 + nrm((L, C_GROUPS, SPATIAL_CHUNK), 0.05),
        'w_out_c': nrm((L, D_C, D_MODEL), D_C ** -0.5),
        'pool_w': nrm((L, len(POOL_WINDOWS), POOL_GROUP_W, POOL_GROUP_W), POOL_GROUP_W ** -0.5),
        'pool_scale': 1.0 + nrm((L, D_D), 0.1),
        'w_out_d': nrm((L, D_D, D_MODEL), D_D ** -0.5),
        'w_o': nrm((L, D_MODEL, D_MODEL), D_MODEL ** -0.5),
        'norm_ffn_g': 1.0 + nrm((L, D_MODEL), 0.05),
        'ffn_w1': nrm((L, D_MODEL, D_FF), D_MODEL ** -0.5),
        'ffn_w3': nrm((L, D_MODEL, D_FF), D_MODEL ** -0.5),
        'ffn_w2': nrm((L, D_FF, D_MODEL), D_FF ** -0.5),
        'norm_final_g': 1.0 + nrm((D_MODEL,), 0.05),
    }


def reference(x_prompt, x_sample, state_conv_a, state_conv_b, state_pool, norm_mix_g, w_in, conv_a_w,
              conv_a_b, ln_a_g, ln_a_b, w_out_a, conv_b_w, w_out_b, ln_c_g, ln_c_b, spatial_w, spatial_b,
              w_out_c, pool_w, pool_scale, w_out_d, w_o, norm_ffn_g, ffn_w1, ffn_w3, ffn_w2, norm_final_g):
    def run_group(x, hist_a, hist_b, hist_d, start):
        new_a, new_b, new_d, new_v = [], [], [], []
        for l in range(DEPTH):
            x, na, nbuf, nd, vr = trunk_layer(
                x, hist_a[l], hist_b[l], hist_d[l], start, norm_mix_g[l], w_in[l], conv_a_w[l], conv_a_b[l],
                ln_a_g[l], ln_a_b[l], w_out_a[l], conv_b_w[l], w_out_b[l], ln_c_g[l], ln_c_b[l], spatial_w[l],
                spatial_b[l], w_out_c[l], pool_w[l], pool_scale[l], w_out_d[l], w_o[l], norm_ffn_g[l],
                ffn_w1[l], ffn_w3[l], ffn_w2[l])
            new_a.append(na)
            new_b.append(nbuf)
            new_d.append(nd)
            new_v.append(vr)
        return (rmsnorm(x, norm_final_g), jnp.stack(new_a), jnp.stack(new_b), jnp.stack(new_d), jnp.stack(new_v))

    dt = x_prompt.dtype
    zero_a = jnp.zeros((DEPTH, BATCH, CONV_A_WIDTH - 1, D_A), dt)
    zero_b = jnp.zeros((DEPTH, BATCH, CONV_B_WIDTH - 1, D_B), dt)
    zero_d = jnp.zeros((DEPTH, BATCH, POOL_HIST, D_D), dt)
    y_prompt, conv_a_prompt, conv_b_prompt, pool_prompt, vrows_prompt = run_group(
        x_prompt, zero_a, zero_b, zero_d, 0)
    y_sample, conv_a_sample, conv_b_sample, pool_sample, vrows_sample = run_group(
        x_sample, state_conv_a, state_conv_b, state_pool, PAST_LEN)
    return (y_prompt, y_sample, conv_a_prompt, conv_a_sample, conv_b_prompt, conv_b_sample,
            pool_prompt, pool_sample, vrows_prompt, vrows_sample)
```

```python
import functools

import jax
import jax.numpy as jnp
import numpy as np
from jax import lax
from jax.experimental import pallas as pl
from jax.experimental.pallas import tpu as pltpu

D_MODEL = 2048
D_BR = D_MODEL // 4
D_FF = 5632
CONV_A_WIDTH = 31
CONV_B_WIDTH = 3
POOL_WINDOWS = (2, 4, 8, 16)
POOL_HIST = 15
POOL_GROUP_W = D_BR // 4
C_GROUPS = 4
C_GROUP_W = D_BR // C_GROUPS
SPATIAL_CHUNK = 128
RMS_EPS = 1e-6
LN_EPS = 1e-5

COL_A = (0, 2 * D_BR)
COL_B = (2 * D_BR, 5 * D_BR)
COL_C = (5 * D_BR, 7 * D_BR)
COL_D = (7 * D_BR, 8 * D_BR)
BR_COLS = 8 * D_BR

PAD_A = 32
PAD_B = 8
PAD_D = 16
ROW_BLOCK = 32

V7X_VMEM_LIMIT = 56 * 1024 * 1024

BF16 = jnp.bfloat16
F32 = jnp.float32


def _dot(a, b):
    return jnp.dot(a, b, preferred_element_type=F32)


def _rmsnorm(x, g):
    ms = jnp.mean(x * x, axis=-1, keepdims=True)
    return x * lax.rsqrt(ms + RMS_EPS) * g


def _layernorm(x, g, b):
    mu = jnp.mean(x, axis=-1, keepdims=True)
    xc = x - mu
    var = jnp.mean(xc * xc, axis=-1, keepdims=True)
    return xc * lax.rsqrt(var + LN_EPS) * g + b


def _silu(x):
    return x * jax.nn.sigmoid(x)


def _gelu_erf(x):
    return 0.5 * x * (1.0 + lax.erf(x * np.float32(np.sqrt(0.5))))


def _windowed_sum(ext_ref, seg, first_row, rows, taps, weights=None, cols=slice(None)):
    outs = []
    rb = min(ROW_BLOCK, rows)
    for r0 in range(0, rows, rb):
        acc = None
        for k in range(taps):
            term = ext_ref[seg, pl.ds(first_row + r0 + k, rb), cols]
            if weights is not None:
                term = term * weights[k]
            acc = term if acc is None else acc + term
        outs.append(acc)
    return outs[0] if len(outs) == 1 else jnp.concatenate(outs, axis=0)


def _mix_kernel(x_ref, g_ref, w_ref, caw_ref, cab_ref, lag_ref, lab_ref, cbw_ref, lcg_ref, lcb_ref,
                wmix_ref, bmix_ref, poolw_ref, pscale_ref, ha_ref, hb_ref, hd_ref,
                h_ref, acts_ref, na_ref, nb_ref, nd_ref, vr_ref,
                ea_ref, eb_ref, ed_ref, *, nseg, seg_len, start, mix_chunk, vrows):
    i = pl.program_id(1)
    L = seg_len
    ha, hb, hd = CONV_A_WIDTH - 1, CONV_B_WIDTH - 1, POOL_HIST

    @pl.when(i == 0)
    def _load_history():
        ea_ref[:, PAD_A - ha:PAD_A, :] = ha_ref[...]
        eb_ref[:, PAD_B - hb:PAD_B, :] = hb_ref[...]
        ed_ref[:, PAD_D - hd:PAD_D, :] = hd_ref[...]

    h = _rmsnorm(x_ref[...], g_ref[...]).astype(BF16)
    h_ref[...] = h

    za = _dot(h, w_ref[:, COL_A[0]:COL_A[1]])
    a = za[:, :D_BR] * jax.nn.sigmoid(za[:, D_BR:])
    caw = [caw_ref[k:k + 1, :] for k in range(CONV_A_WIDTH)]
    for s in range(nseg):
        ea_ref[s, PAD_A:PAD_A + L, :] = a[s * L:(s + 1) * L]
        conv = _windowed_sum(ea_ref, s, PAD_A - ha, L, CONV_A_WIDTH, caw)
        a_act = _silu(_layernorm(conv + cab_ref[...], lag_ref[...], lab_ref[...]))
        acts_ref[s * L:(s + 1) * L, 0:D_BR] = a_act.astype(BF16)
        tail = ea_ref[s, PAD_A + L - ha:PAD_A + L, :]
        na_ref[s] = tail
        ea_ref[s, PAD_A - ha:PAD_A, :] = tail

    zb = _dot(h, w_ref[:, COL_B[0]:COL_B[1]])
    m = zb[:, D_BR:2 * D_BR] * zb[:, 2 * D_BR:]
    cbw = [cbw_ref[k:k + 1, :] for k in range(CONV_B_WIDTH)]
    for s in range(nseg):
        eb_ref[s, PAD_B:PAD_B + L, :] = m[s * L:(s + 1) * L]
        conv = _windowed_sum(eb_ref, s, PAD_B - hb, L, CONV_B_WIDTH, cbw)
        acts_ref[s * L:(s + 1) * L, D_BR:2 * D_BR] = (zb[s * L:(s + 1) * L, :D_BR] * conv).astype(BF16)
        tail = eb_ref[s, PAD_B + L - hb:PAD_B + L, :]
        nb_ref[s] = tail
        eb_ref[s, PAD_B - hb:PAD_B, :] = tail

    zc = _gelu_erf(_dot(h, w_ref[:, COL_C[0]:COL_C[1]]))
    u = zc[:, :D_BR]
    v = _layernorm(zc[:, D_BR:], lcg_ref[...], lcb_ref[...])
    for s in range(nseg):
        vr_ref[s] = v[(s + 1) * L - vrows:(s + 1) * L]
    vb = v.astype(BF16)
    rows = nseg * L
    for c0 in range(0, rows, mix_chunk):
        for g in range(C_GROUPS):
            cols = slice(g * C_GROUP_W, (g + 1) * C_GROUP_W)
            mixed = _dot(wmix_ref[g], vb[c0:c0 + mix_chunk, cols]) + bmix_ref[:, cols]
            acts_ref[c0:c0 + mix_chunk, 2 * D_BR + g * C_GROUP_W:2 * D_BR + (g + 1) * C_GROUP_W] = (
                u[c0:c0 + mix_chunk, cols] * mixed).astype(BF16)

    zd = _dot(h, w_ref[:, COL_D[0]:COL_D[1]])
    for s in range(nseg):
        ed_ref[s, PAD_D:PAD_D + L, :] = zd[s * L:(s + 1) * L]
        pos = start + i * L + lax.broadcasted_iota(jnp.int32, (L, POOL_GROUP_W), 0)
        for g, wdw in enumerate(POOL_WINDOWS):
            cols = slice(g * POOL_GROUP_W, (g + 1) * POOL_GROUP_W)
            ssum = _windowed_sum(ed_ref, s, PAD_D - (wdw - 1), L, wdw, None, cols)
            cnt = jnp.minimum(pos + 1, wdw).astype(F32)
            pooled = ssum / cnt - zd[s * L:(s + 1) * L, cols]
            d_mix = _dot(pooled.astype(BF16), poolw_ref[g])
            acts_ref[s * L:(s + 1) * L, 3 * D_BR + g * POOL_GROUP_W:3 * D_BR + (g + 1) * POOL_GROUP_W] = (
                d_mix * pscale_ref[:, cols]).astype(BF16)
        tail = ed_ref[s, PAD_D + L - hd:PAD_D + L, :]
        nd_ref[s] = tail
        ed_ref[s, PAD_D - hd:PAD_D, :] = tail


def _mix_call(x, hist_a, hist_b, hist_d, lw, *, nseg, seg_len, start):
    nb, t, _ = x.shape
    tm = nseg * seg_len
    n_tiles = t // tm
    assert t % tm == 0 and (nseg == 1 or n_tiles == 1)
    mix_chunk = lw['wmix'].shape[-1]
    assert tm % mix_chunk == 0
    vrows = min(seg_len, SPATIAL_CHUNK)

    def full(arr):
        return pl.BlockSpec(arr.shape, lambda b, i, _n=arr.ndim: (0,) * _n)

    def per_batch(arr):
        return pl.BlockSpec((None,) + arr.shape[1:], lambda b, i, _n=arr.ndim: (b,) + (0,) * (_n - 1))

    tile = pl.BlockSpec((None, tm, D_MODEL), lambda b, i: (b, i, 0))
    small = [lw['conv_a_w'], lw['conv_a_b'], lw['ln_a_g'], lw['ln_a_b'], lw['conv_b_w'], lw['ln_c_g'],
             lw['ln_c_b'], lw['wmix'], lw['bmix'], lw['pool_w'], lw['pool_scale']]
    out_shape = (
        jax.ShapeDtypeStruct((nb, t, D_MODEL), BF16),
        jax.ShapeDtypeStruct((nb, t, D_MODEL), BF16),
        jax.ShapeDtypeStruct((nb, nseg, CONV_A_WIDTH - 1, D_BR), F32),
        jax.ShapeDtypeStruct((nb, nseg, CONV_B_WIDTH - 1, D_BR), F32),
        jax.ShapeDtypeStruct((nb, nseg, POOL_HIST, D_BR), F32),
        jax.ShapeDtypeStruct((nb, nseg, vrows, D_BR), F32),
    )
    kern = functools.partial(_mix_kernel, nseg=nseg, seg_len=seg_len, start=start, mix_chunk=mix_chunk,
                             vrows=vrows)
    return pl.pallas_call(
        kern,
        grid=(nb, n_tiles),
        in_specs=[tile, full(lw['norm_mix_g']), full(lw['w_br'])] + [full(a) for a in small]
        + [per_batch(hist_a), per_batch(hist_b), per_batch(hist_d)],
        out_specs=(tile, tile) + tuple(per_batch(s) for s in out_shape[2:]),
        out_shape=out_shape,
        scratch_shapes=[
            pltpu.VMEM((nseg, PAD_A + seg_len, D_BR), F32),
            pltpu.VMEM((nseg, PAD_B + seg_len, D_BR), F32),
            pltpu.VMEM((nseg, PAD_D + seg_len, D_BR), F32),
        ],
        compiler_params=pltpu.CompilerParams(dimension_semantics=("arbitrary", "arbitrary"),
                                             vmem_limit_bytes=V7X_VMEM_LIMIT),
        name="mix",
    )(x, lw['norm_mix_g'], lw['w_br'], *small, hist_a, hist_b, hist_d)


def _gate_kernel(h_ref, acts_ref, wg0, wg1, wg2, wg3, wo0, wo1, wo2, wo3, out_ref):
    h = h_ref[...]
    acc = None
    for b, (wg, wo) in enumerate(((wg0, wo0), (wg1, wo1), (wg2, wo2), (wg3, wo3))):
        gate = jax.nn.sigmoid(_dot(h, wg[...]))
        term = gate * _dot(acts_ref[:, b * D_BR:(b + 1) * D_BR], wo[...])
        acc = term if acc is None else acc + term
    out_ref[...] = acc.astype(BF16)


def _gate_call(h, acts, lw, *, tm, nc):
    n = h.shape[0]
    n_col = D_MODEL // nc
    act_tile = pl.BlockSpec((tm, D_MODEL), lambda i, j: (i, 0))
    wg_specs = [pl.BlockSpec((D_MODEL, nc), lambda i, j, _b=b: (0, _b * n_col + j)) for b in range(4)]
    wo_spec = pl.BlockSpec((D_BR, nc), lambda i, j: (0, j))
    return pl.pallas_call(
        _gate_kernel,
        grid=(n // tm, n_col),
        in_specs=[act_tile, act_tile] + wg_specs + [wo_spec] * 4,
        out_specs=pl.BlockSpec((tm, nc), lambda i, j: (i, j)),
        out_shape=jax.ShapeDtypeStruct((n, D_MODEL), BF16),
        compiler_params=pltpu.CompilerParams(dimension_semantics=("parallel", "arbitrary"),
                                             vmem_limit_bytes=V7X_VMEM_LIMIT),
        name="gate",
    )(h, acts, lw['w_gate'], lw['w_gate'], lw['w_gate'], lw['w_gate'],
      lw['w_out_a'], lw['w_out_b'], lw['w_out_c'], lw['w_out_d'])


def _oproj_kernel(x_ref, m_ref, wo_ref, out_ref):
    out_ref[...] = x_ref[...] + _dot(m_ref[...], wo_ref[...])


def _oproj_call(x, merged, lw, *, tm):
    n = x.shape[0]
    return pl.pallas_call(
        _oproj_kernel,
        grid=(n // tm,),
        in_specs=[pl.BlockSpec((tm, D_MODEL), lambda i: (i, 0)),
                  pl.BlockSpec((tm, D_MODEL), lambda i: (i, 0)),
                  pl.BlockSpec((D_MODEL, D_MODEL), lambda i: (0, 0))],
        out_specs=pl.BlockSpec((tm, D_MODEL), lambda i: (i, 0)),
        out_shape=jax.ShapeDtypeStruct((n, D_MODEL), F32),
        compiler_params=pltpu.CompilerParams(dimension_semantics=("parallel",),
                                             vmem_limit_bytes=V7X_VMEM_LIMIT),
        name="oproj",
    )(x, merged, lw['w_o'])


def _ffn_kernel(x_ref, g_ref, w1_ref, w3_ref, w2_ref, gfin_ref, out_ref, h2_ref, *, final_norm):
    j = pl.program_id(1)

    @pl.when(j == 0)
    def _start():
        x = x_ref[...]
        h2_ref[...] = _rmsnorm(x, g_ref[...]).astype(BF16)
        out_ref[...] = x

    h2 = h2_ref[...]
    hidden = (_silu(_dot(h2, w1_ref[...])) * _dot(h2, w3_ref[...])).astype(BF16)
    out_ref[...] += _dot(hidden, w2_ref[...])

    if final_norm:
        @pl.when(j == pl.num_programs(1) - 1)
        def _finish():
            out_ref[...] = _rmsnorm(out_ref[...], gfin_ref[...])


def _ffn_call(x, lw, norm_final_g, *, tm, fc, final_norm):
    n = x.shape[0]
    kern = functools.partial(_ffn_kernel, final_norm=final_norm)
    vec = pl.BlockSpec((1, D_MODEL), lambda i, j: (0, 0))
    return pl.pallas_call(
        kern,
        grid=(n // tm, D_FF // fc),
        in_specs=[pl.BlockSpec((tm, D_MODEL), lambda i, j: (i, 0)), vec,
                  pl.BlockSpec((D_MODEL, fc), lambda i, j: (0, j)),
                  pl.BlockSpec((D_MODEL, fc), lambda i, j: (0, j)),
                  pl.BlockSpec((fc, D_MODEL), lambda i, j: (j, 0)), vec],
        out_specs=pl.BlockSpec((tm, D_MODEL), lambda i, j: (i, 0)),
        out_shape=jax.ShapeDtypeStruct((n, D_MODEL), F32),
        scratch_shapes=[pltpu.VMEM((tm, D_MODEL), BF16)],
        compiler_params=pltpu.CompilerParams(dimension_semantics=("parallel", "arbitrary"),
                                             vmem_limit_bytes=V7X_VMEM_LIMIT),
        name="ffn",
    )(x, lw['norm_ffn_g'], lw['ffn_w1'], lw['ffn_w3'], lw['ffn_w2'], norm_final_g)


def _spatial_mix_operands(spatial_w, spatial_b, seg_len):
    tri = jnp.tril(jnp.ones((SPATIAL_CHUNK, SPATIAL_CHUNK), dtype=bool))
    wm = jnp.where(tri[None], spatial_w, jnp.zeros((), spatial_w.dtype))
    bias = jnp.repeat(spatial_b.T, C_GROUP_W, axis=1)
    if seg_len >= SPATIAL_CHUNK:
        assert seg_len % SPATIAL_CHUNK == 0
        return wm.astype(BF16), bias
    reps = 2 * SPATIAL_CHUNK // seg_len
    eye = jnp.eye(reps, dtype=wm.dtype)
    corner = wm[:, :seg_len, :seg_len]
    blockdiag = jnp.einsum('pq,gts->gptqs', eye, corner).reshape(C_GROUPS, reps * seg_len, reps * seg_len)
    return blockdiag.astype(BF16), jnp.tile(bias[:seg_len], (reps, 1))


PROMPT_MIX_TILE = 512
PROMPT_TILE = 1024
GATE_COLS = 512
FFN_COLS = 512


def kernel(x_prompt, x_sample, state_conv_a, state_conv_b, state_pool, norm_mix_g, w_in, conv_a_w, conv_a_b,
           ln_a_g, ln_a_b, w_out_a, conv_b_w, w_out_b, ln_c_g, ln_c_b, spatial_w, spatial_b, w_out_c, pool_w,
           pool_scale, w_out_d, w_o, norm_ffn_g, ffn_w1, ffn_w3, ffn_w2, norm_final_g):
    depth = w_in.shape[0]
    nbp, seq, _ = x_prompt.shape
    nbs, dseq, _ = x_sample.shape
    past_len = 4096

    row = lambda v: v.reshape(1, -1)
    layers = []
    for l in range(depth):
        lw = dict(
            norm_mix_g=row(norm_mix_g[l]),
            w_br=w_in[l][:, :BR_COLS].astype(BF16),
            w_gate=w_in[l][:, BR_COLS:].astype(BF16),
            conv_a_w=conv_a_w[l], conv_a_b=row(conv_a_b[l]), ln_a_g=row(ln_a_g[l]), ln_a_b=row(ln_a_b[l]),
            conv_b_w=conv_b_w[l], ln_c_g=row(ln_c_g[l]), ln_c_b=row(ln_c_b[l]),
            pool_w=pool_w[l].astype(BF16), pool_scale=row(pool_scale[l]),
            w_out_a=w_out_a[l].astype(BF16), w_out_b=w_out_b[l].astype(BF16),
            w_out_c=w_out_c[l].astype(BF16), w_out_d=w_out_d[l].astype(BF16),
            w_o=w_o[l].astype(BF16), norm_ffn_g=row(norm_ffn_g[l]),
            ffn_w1=ffn_w1[l].astype(BF16), ffn_w3=ffn_w3[l].astype(BF16), ffn_w2=ffn_w2[l].astype(BF16),
        )
        layers.append(lw)
    gfin = row(norm_final_g)

    def run_group(x, hist_a, hist_b, hist_d, *, nseg, seg_len, start, tm):
        nb, t, _ = x.shape
        states = []
        for l, lw in enumerate(layers):
            wmix, bmix = _spatial_mix_operands(spatial_w[l], spatial_b[l], seg_len)
            lw = dict(lw, wmix=wmix, bmix=bmix)
            h, acts, na, nbuf, nd, vr = _mix_call(x, hist_a[l], hist_b[l], hist_d[l], lw,
                                                  nseg=nseg, seg_len=seg_len, start=start)
            xf = x.reshape(nb * t, D_MODEL)
            merged = _gate_call(h.reshape(nb * t, D_MODEL), acts.reshape(nb * t, D_MODEL), lw,
                                tm=tm, nc=GATE_COLS)
            x1 = _oproj_call(xf, merged, lw, tm=min(tm, 512))
            x = _ffn_call(x1, lw, gfin, tm=tm, fc=FFN_COLS, final_norm=(l == depth - 1)).reshape(nb, t, D_MODEL)
            states.append((na, nbuf, nd, vr))
        return (x,) + tuple(jnp.stack(s) for s in zip(*states))

    dt = x_prompt.dtype
    zeros = lambda rows: jnp.zeros((depth, nbp, 1, rows, D_BR), dt)
    yp, ap, bp, dp, vp = run_group(x_prompt, zeros(CONV_A_WIDTH - 1), zeros(CONV_B_WIDTH - 1), zeros(POOL_HIST),
                                   nseg=1, seg_len=PROMPT_MIX_TILE, start=0, tm=PROMPT_TILE)
    ys, a_s, b_s, d_s, v_s = run_group(
        x_sample.reshape(1, nbs * dseq, D_MODEL), state_conv_a[:, None], state_conv_b[:, None],
        state_pool[:, None], nseg=nbs, seg_len=dseq, start=past_len, tm=nbs * dseq)

    squeeze_p = lambda s: s.reshape(depth, nbp, s.shape[-2], D_BR)
    squeeze_s = lambda s: s.reshape(depth, nbs, s.shape[-2], D_BR)
    return (yp, ys.reshape(nbs, dseq, D_MODEL), squeeze_p(ap), squeeze_s(a_s), squeeze_p(bp), squeeze_s(b_s),
            squeeze_p(dp), squeeze_s(d_s), squeeze_p(vp), squeeze_s(v_s))
```

```python
import functools

import jax
import jax.numpy as jnp
import numpy as np
from jax import lax
from jax.experimental import pallas as pl
from jax.experimental.pallas import tpu as pltpu

D_MODEL = 2048
D_BR = D_MODEL // 4
D_FF = 5632
CONV_A_WIDTH = 31
CONV_B_WIDTH = 3
POOL_WINDOWS = (2, 4, 8, 16)
POOL_HIST = 15
POOL_GROUP_W = D_BR // 4
C_GROUPS = 4
C_GROUP_W = D_BR // C_GROUPS
SPATIAL_CHUNK = 128
PAST_LEN = 4096
RMS_EPS = 1e-6
LN_EPS = 1e-5

COL_A = (0, 2 * D_BR)
COL_B = (2 * D_BR, 5 * D_BR)
COL_C = (5 * D_BR, 7 * D_BR)
COL_D = (7 * D_BR, 8 * D_BR)
BR_COLS = 8 * D_BR

PAD_A = 32
PAD_B = 8
PAD_D = 16
SUBLANES = 8
ROW_BLOCK = 32

V7X_VMEM_LIMIT = 56 * 1024 * 1024

BF16 = jnp.bfloat16
F32 = jnp.float32


def _dot(a, b):
    return jnp.dot(a, b, preferred_element_type=F32)


def _rmsnorm(x, g):
    ms = jnp.mean(x * x, axis=-1, keepdims=True)
    return x * lax.rsqrt(ms + RMS_EPS) * g


def _layernorm(x, g, b):
    mu = jnp.mean(x, axis=-1, keepdims=True)
    xc = x - mu
    var = jnp.mean(xc * xc, axis=-1, keepdims=True)
    return xc * lax.rsqrt(var + LN_EPS) * g + b


def _silu(x):
    return x * jax.nn.sigmoid(x)


def _gelu_erf(x):
    return 0.5 * x * (1.0 + lax.erf(x * np.float32(np.sqrt(0.5))))


def _windowed_sum(ext_ref, seg, first_row, rows, taps, weights=None, cols=slice(None)):
    outs = []
    rb = min(ROW_BLOCK, rows)
    for r0 in range(0, rows, rb):
        acc = None
        for k in range(taps):
            term = ext_ref[seg, pl.ds(first_row + r0 + k, rb), cols]
            if weights is not None:
                term = term * weights[k]
            acc = term if acc is None else acc + term
        outs.append(acc)
    return outs[0] if len(outs) == 1 else jnp.concatenate(outs, axis=0)


def _conv_a(ea_ref, sh_ref, seg, weights, p0, rows):
    ha = CONV_A_WIDTH - 1
    halo = PAD_A - SUBLANES
    lo = 0 if p0 == 0 else p0 + halo
    hi = p0 + rows + halo
    for r in range(1, SUBLANES):
        sh_ref[r - 1, seg, lo:hi, :] = ea_ref[seg, pl.ds(lo + r, hi - lo), :]
    outs = []
    rb = min(ROW_BLOCK, rows)
    for r0 in range(p0, p0 + rows, rb):
        acc = None
        for k in range(CONV_A_WIDTH):
            first = PAD_A - ha + k
            r = first % SUBLANES
            if r == 0:
                term = ea_ref[seg, pl.ds(first + r0, rb), :]
            else:
                term = sh_ref[r - 1, seg, pl.ds(first - r + r0, rb), :]
            term = term * weights[k]
            acc = term if acc is None else acc + term
        outs.append(acc)
    return outs[0] if len(outs) == 1 else jnp.concatenate(outs, axis=0)


def _mix_kernel(x_ref, g_ref, w_ref, caw_ref, cab_ref, lag_ref, lab_ref, cbw_ref, lcg_ref, lcb_ref,
                wmix_ref, bmix_ref, poolw_ref, pscale_ref, ha_ref, hb_ref, hd_ref,
                h_ref, acts_ref, na_ref, nb_ref, nd_ref, vr_ref,
                ea_ref, eb_ref, ed_ref, sh_ref, *, nseg, seg_len, n_sub, start, mix_chunk, vrows):
    i = pl.program_id(1)
    L = seg_len
    R = nseg * L // n_sub
    ha, hb, hd = CONV_A_WIDTH - 1, CONV_B_WIDTH - 1, POOL_HIST

    @pl.when(i == 0)
    def _load_history():
        ea_ref[:, PAD_A - ha:PAD_A, :] = ha_ref[...]
        eb_ref[:, PAD_B - hb:PAD_B, :] = hb_ref[...]
        ed_ref[:, PAD_D - hd:PAD_D, :] = hd_ref[...]

    caw = [caw_ref[k:k + 1, :] for k in range(CONV_A_WIDTH)]
    cbw = [cbw_ref[k:k + 1, :] for k in range(CONV_B_WIDTH)]

    for sub in range(n_sub):
        t0 = sub * R
        pieces = [(0, t0, 0, R)] if nseg == 1 else [(s, 0, s * L, L) for s in range(nseg)]

        h = _rmsnorm(x_ref[t0:t0 + R, :], g_ref[...]).astype(BF16)
        h_ref[t0:t0 + R, :] = h

        za = _dot(h, w_ref[:, COL_A[0]:COL_A[1]])
        a = za[:, :D_BR] * jax.nn.sigmoid(za[:, D_BR:])
        for s, p0, q0, n in pieces:
            ea_ref[s, PAD_A + p0:PAD_A + p0 + n, :] = a[q0:q0 + n]
            conv = _conv_a(ea_ref, sh_ref, s, caw, p0, n)
            a_act = _silu(_layernorm(conv + cab_ref[...], lag_ref[...], lab_ref[...]))
            acts_ref[t0 + q0:t0 + q0 + n, 0:D_BR] = a_act.astype(BF16)

        zb = _dot(h, w_ref[:, COL_B[0]:COL_B[1]])
        m = zb[:, D_BR:2 * D_BR] * zb[:, 2 * D_BR:]
        for s, p0, q0, n in pieces:
            eb_ref[s, PAD_B + p0:PAD_B + p0 + n, :] = m[q0:q0 + n]
            conv = _windowed_sum(eb_ref, s, PAD_B - hb + p0, n, CONV_B_WIDTH, cbw)
            acts_ref[t0 + q0:t0 + q0 + n, D_BR:2 * D_BR] = (zb[q0:q0 + n, :D_BR] * conv).astype(BF16)

        zc = _gelu_erf(_dot(h, w_ref[:, COL_C[0]:COL_C[1]]))
        u = zc[:, :D_BR]
        v = _layernorm(zc[:, D_BR:], lcg_ref[...], lcb_ref[...])
        for s, p0, q0, n in pieces:
            if p0 + n == L:
                vr_ref[s] = v[q0 + n - vrows:q0 + n]
        vb = v.astype(BF16)
        for c0 in range(0, R, mix_chunk):
            for g in range(C_GROUPS):
                cols = slice(g * C_GROUP_W, (g + 1) * C_GROUP_W)
                mixed = _dot(wmix_ref[g], vb[c0:c0 + mix_chunk, cols]) + bmix_ref[:, cols]
                acts_ref[t0 + c0:t0 + c0 + mix_chunk,
                         2 * D_BR + g * C_GROUP_W:2 * D_BR + (g + 1) * C_GROUP_W] = (
                    u[c0:c0 + mix_chunk, cols] * mixed).astype(BF16)

        zd = _dot(h, w_ref[:, COL_D[0]:COL_D[1]])
        for s, p0, q0, n in pieces:
            ed_ref[s, PAD_D + p0:PAD_D + p0 + n, :] = zd[q0:q0 + n]
            pos = start + i * L + p0 + lax.broadcasted_iota(jnp.int32, (n, POOL_GROUP_W), 0)
            for g, wdw in enumerate(POOL_WINDOWS):
                cols = slice(g * POOL_GROUP_W, (g + 1) * POOL_GROUP_W)
                ssum = _windowed_sum(ed_ref, s, PAD_D - (wdw - 1) + p0, n, wdw, None, cols)
                cnt = jnp.minimum(pos + 1, wdw).astype(F32)
                pooled = ssum / cnt - zd[q0:q0 + n, cols]
                d_mix = _dot(pooled.astype(BF16), poolw_ref[g])
                acts_ref[t0 + q0:t0 + q0 + n,
                         3 * D_BR + g * POOL_GROUP_W:3 * D_BR + (g + 1) * POOL_GROUP_W] = (
                    d_mix * pscale_ref[:, cols]).astype(BF16)

    for s in range(nseg):
        for ext_ref, new_ref, pad, hist in ((ea_ref, na_ref, PAD_A, ha), (eb_ref, nb_ref, PAD_B, hb),
                                            (ed_ref, nd_ref, PAD_D, hd)):
            tail = ext_ref[s, pad + L - hist:pad + L, :]
            new_ref[s] = tail
            ext_ref[s, pad - hist:pad, :] = tail


def _mix_call(x, hist_a, hist_b, hist_d, wts, wmix, bmix, *, layer, nseg, seg_len, n_sub, start):
    nb, t, _ = x.shape
    tm = nseg * seg_len
    n_tiles = t // tm
    assert t % tm == 0 and (nseg == 1 or (n_tiles == 1 and n_sub == 1))
    mix_chunk = wmix.shape[-1]
    assert tm % n_sub == 0 and (tm // n_sub) % mix_chunk == 0
    vrows = min(seg_len, SPATIAL_CHUNK)
    assert vrows <= tm // n_sub

    def of_layer(arr):
        return pl.BlockSpec((None,) + arr.shape[1:], lambda b, i, _n=arr.ndim: (layer,) + (0,) * (_n - 1))

    def per_batch(arr):
        return pl.BlockSpec((None,) + arr.shape[1:], lambda b, i, _n=arr.ndim: (b,) + (0,) * (_n - 1))

    tile = pl.BlockSpec((None, tm, D_MODEL), lambda b, i: (b, i, 0))
    w_br_spec = pl.BlockSpec((None, D_MODEL, BR_COLS), lambda b, i: (layer, 0, 0),
                             pipeline_mode=pl.Buffered(1))
    small = [wts['conv_a_w'], wts['conv_a_b'], wts['ln_a_g'], wts['ln_a_b'], wts['conv_b_w'], wts['ln_c_g'],
             wts['ln_c_b'], wmix, bmix, wts['pool_w'], wts['pool_scale']]
    out_shape = (
        jax.ShapeDtypeStruct((nb, t, D_MODEL), BF16),
        jax.ShapeDtypeStruct((nb, t, D_MODEL), BF16),
        jax.ShapeDtypeStruct((nb, nseg, CONV_A_WIDTH - 1, D_BR), F32),
        jax.ShapeDtypeStruct((nb, nseg, CONV_B_WIDTH - 1, D_BR), F32),
        jax.ShapeDtypeStruct((nb, nseg, POOL_HIST, D_BR), F32),
        jax.ShapeDtypeStruct((nb, nseg, vrows, D_BR), F32),
    )
    kern = functools.partial(_mix_kernel, nseg=nseg, seg_len=seg_len, n_sub=n_sub, start=start,
                             mix_chunk=mix_chunk, vrows=vrows)
    return pl.pallas_call(
        kern,
        grid=(nb, n_tiles),
        in_specs=[tile, of_layer(wts['norm_mix_g']), w_br_spec] + [of_layer(a) for a in small]
        + [per_batch(hist_a), per_batch(hist_b), per_batch(hist_d)],
        out_specs=(tile, tile) + tuple(per_batch(s) for s in out_shape[2:]),
        out_shape=out_shape,
        scratch_shapes=[
            pltpu.VMEM((nseg, PAD_A + seg_len, D_BR), F32),
            pltpu.VMEM((nseg, PAD_B + seg_len, D_BR), F32),
            pltpu.VMEM((nseg, PAD_D + seg_len, D_BR), F32),
            pltpu.VMEM((SUBLANES - 1, nseg, PAD_A - SUBLANES + seg_len, D_BR), F32),
        ],
        compiler_params=pltpu.CompilerParams(dimension_semantics=("arbitrary", "arbitrary"),
                                             vmem_limit_bytes=V7X_VMEM_LIMIT),
        name="mix",
    )(x, wts['norm_mix_g'], wts['w_in'], *small, hist_a, hist_b, hist_d)


def _gate_kernel(h_ref, acts_ref, wg0, wg1, wg2, wg3, wo0, wo1, wo2, wo3, out_ref):
    h = h_ref[...]
    acc = None
    for b, (wg, wo) in enumerate(((wg0, wo0), (wg1, wo1), (wg2, wo2), (wg3, wo3))):
        gate = jax.nn.sigmoid(_dot(h, wg[...]))
        term = gate * _dot(acts_ref[:, b * D_BR:(b + 1) * D_BR], wo[...])
        acc = term if acc is None else acc + term
    out_ref[...] = acc.astype(BF16)


def _gate_call(h, acts, wts, *, layer, tm, nc):
    n = h.shape[0]
    n_col = D_MODEL // nc
    gate0 = BR_COLS // nc
    act_tile = pl.BlockSpec((tm, D_MODEL), lambda i, j: (i, 0))
    wg_specs = [pl.BlockSpec((None, D_MODEL, nc), lambda i, j, _b=b: (layer, 0, gate0 + _b * n_col + j))
                for b in range(4)]
    wo_spec = pl.BlockSpec((None, D_BR, nc), lambda i, j: (layer, 0, j))
    return pl.pallas_call(
        _gate_kernel,
        grid=(n // tm, n_col),
        in_specs=[act_tile, act_tile] + wg_specs + [wo_spec] * 4,
        out_specs=pl.BlockSpec((tm, nc), lambda i, j: (i, j)),
        out_shape=jax.ShapeDtypeStruct((n, D_MODEL), BF16),
        compiler_params=pltpu.CompilerParams(dimension_semantics=("parallel", "arbitrary"),
                                             vmem_limit_bytes=V7X_VMEM_LIMIT),
        name="gate",
    )(h, acts, wts['w_in'], wts['w_in'], wts['w_in'], wts['w_in'],
      wts['w_out_a'], wts['w_out_b'], wts['w_out_c'], wts['w_out_d'])


def _oproj_kernel(x_ref, m_ref, wo_ref, out_ref):
    out_ref[...] = x_ref[...] + _dot(m_ref[...], wo_ref[...])


def _oproj_call(x, merged, wts, *, layer, tm):
    n = x.shape[0]
    return pl.pallas_call(
        _oproj_kernel,
        grid=(n // tm,),
        in_specs=[pl.BlockSpec((tm, D_MODEL), lambda i: (i, 0)),
                  pl.BlockSpec((tm, D_MODEL), lambda i: (i, 0)),
                  pl.BlockSpec((None, D_MODEL, D_MODEL), lambda i: (layer, 0, 0),
                               pipeline_mode=pl.Buffered(1))],
        out_specs=pl.BlockSpec((tm, D_MODEL), lambda i: (i, 0)),
        out_shape=jax.ShapeDtypeStruct((n, D_MODEL), F32),
        compiler_params=pltpu.CompilerParams(dimension_semantics=("parallel",),
                                             vmem_limit_bytes=V7X_VMEM_LIMIT),
        name="oproj",
    )(x, merged, wts['w_o'])


def _ffn_kernel(x_ref, g_ref, w1_ref, w3_ref, w2_ref, gfin_ref, out_ref, h2_ref, *, final_norm):
    j = pl.program_id(1)

    @pl.when(j == 0)
    def _start():
        x = x_ref[...]
        h2_ref[...] = _rmsnorm(x, g_ref[...]).astype(BF16)
        out_ref[...] = x

    h2 = h2_ref[...]
    hidden = (_silu(_dot(h2, w1_ref[...])) * _dot(h2, w3_ref[...])).astype(BF16)
    out_ref[...] += _dot(hidden, w2_ref[...])

    if final_norm:
        @pl.when(j == pl.num_programs(1) - 1)
        def _finish():
            out_ref[...] = _rmsnorm(out_ref[...], gfin_ref[...])


def _ffn_call(x, wts, norm_final_g, *, layer, tm, fc, final_norm):
    n = x.shape[0]
    kern = functools.partial(_ffn_kernel, final_norm=final_norm)
    return pl.pallas_call(
        kern,
        grid=(n // tm, D_FF // fc),
        in_specs=[pl.BlockSpec((tm, D_MODEL), lambda i, j: (i, 0)),
                  pl.BlockSpec((None, 1, D_MODEL), lambda i, j: (layer, 0, 0)),
                  pl.BlockSpec((None, D_MODEL, fc), lambda i, j: (layer, 0, j)),
                  pl.BlockSpec((None, D_MODEL, fc), lambda i, j: (layer, 0, j)),
                  pl.BlockSpec((None, fc, D_MODEL), lambda i, j: (layer, j, 0)),
                  pl.BlockSpec((1, D_MODEL), lambda i, j: (0, 0))],
        out_specs=pl.BlockSpec((tm, D_MODEL), lambda i, j: (i, 0)),
        out_shape=jax.ShapeDtypeStruct((n, D_MODEL), F32),
        scratch_shapes=[pltpu.VMEM((tm, D_MODEL), BF16)],
        compiler_params=pltpu.CompilerParams(dimension_semantics=("parallel", "arbitrary"),
                                             vmem_limit_bytes=V7X_VMEM_LIMIT),
        name="ffn",
    )(x, wts['norm_ffn_g'], wts['ffn_w1'], wts['ffn_w3'], wts['ffn_w2'], norm_final_g)


def _spatial_mix_operands(spatial_w, spatial_b, seg_len):
    depth = spatial_w.shape[0]
    tri = jnp.tril(jnp.ones((SPATIAL_CHUNK, SPATIAL_CHUNK), dtype=bool))
    wm = jnp.where(tri, spatial_w, jnp.zeros((), spatial_w.dtype))
    bias = jnp.repeat(jnp.swapaxes(spatial_b, 1, 2), C_GROUP_W, axis=2)
    if seg_len >= SPATIAL_CHUNK:
        assert seg_len % SPATIAL_CHUNK == 0
        return wm.astype(BF16), bias
    reps = 2 * SPATIAL_CHUNK // seg_len
    eye = jnp.eye(reps, dtype=wm.dtype)
    corner = wm[:, :, :seg_len, :seg_len]
    blockdiag = jnp.einsum('pq,lgts->lgptqs', eye, corner).reshape(
        depth, C_GROUPS, reps * seg_len, reps * seg_len)
    return blockdiag.astype(BF16), jnp.tile(bias[:, :seg_len], (1, reps, 1))


PROMPT_MIX_TILE = 512
PROMPT_MIX_SLABS = 1
PROMPT_TILE = 1024
GATE_COLS = 512
FFN_COLS = 512


def kernel(x_prompt, x_sample, state_conv_a, state_conv_b, state_pool, norm_mix_g, w_in, conv_a_w, conv_a_b,
           ln_a_g, ln_a_b, w_out_a, conv_b_w, w_out_b, ln_c_g, ln_c_b, spatial_w, spatial_b, w_out_c, pool_w,
           pool_scale, w_out_d, w_o, norm_ffn_g, ffn_w1, ffn_w3, ffn_w2, norm_final_g):
    depth = w_in.shape[0]
    nbp, seq, _ = x_prompt.shape
    nbs, dseq, _ = x_sample.shape

    rows = lambda v: v.reshape(depth, 1, -1)
    wts = dict(
        w_in=w_in.astype(BF16),
        w_out_a=w_out_a.astype(BF16), w_out_b=w_out_b.astype(BF16),
        w_out_c=w_out_c.astype(BF16), w_out_d=w_out_d.astype(BF16),
        w_o=w_o.astype(BF16), pool_w=pool_w.astype(BF16),
        ffn_w1=ffn_w1.astype(BF16), ffn_w3=ffn_w3.astype(BF16), ffn_w2=ffn_w2.astype(BF16),
        norm_mix_g=rows(norm_mix_g), norm_ffn_g=rows(norm_ffn_g),
        conv_a_w=conv_a_w, conv_a_b=rows(conv_a_b), ln_a_g=rows(ln_a_g), ln_a_b=rows(ln_a_b),
        conv_b_w=conv_b_w, ln_c_g=rows(ln_c_g), ln_c_b=rows(ln_c_b), pool_scale=rows(pool_scale),
    )
    gfin = norm_final_g.reshape(1, -1)

    def run_group(x, hist_a, hist_b, hist_d, *, nseg, seg_len, n_sub, start, tm):
        nb, t, _ = x.shape
        wmix, bmix = _spatial_mix_operands(spatial_w, spatial_b, seg_len)
        states = []
        for l in range(depth):
            h, acts, na, nbuf, nd, vr = _mix_call(x, hist_a[l], hist_b[l], hist_d[l], wts, wmix, bmix,
                                                  layer=l, nseg=nseg, seg_len=seg_len, n_sub=n_sub,
                                                  start=start)
            xf = x.reshape(nb * t, D_MODEL)
            merged = _gate_call(h.reshape(nb * t, D_MODEL), acts.reshape(nb * t, D_MODEL), wts,
                                layer=l, tm=tm, nc=GATE_COLS)
            x1 = _oproj_call(xf, merged, wts, layer=l, tm=min(tm, 512))
            x = _ffn_call(x1, wts, gfin, layer=l, tm=tm, fc=FFN_COLS,
                          final_norm=(l == depth - 1)).reshape(nb, t, D_MODEL)
            states.append((na, nbuf, nd, vr))
        return (x,) + tuple(jnp.stack(s) for s in zip(*states))

    dt = x_prompt.dtype
    zeros = lambda rows: jnp.zeros((depth, nbp, 1, rows, D_BR), dt)
    yp, ap, bp, dp, vp = run_group(x_prompt, zeros(CONV_A_WIDTH - 1), zeros(CONV_B_WIDTH - 1), zeros(POOL_HIST),
                                   nseg=1, seg_len=PROMPT_MIX_TILE, n_sub=PROMPT_MIX_SLABS, start=0,
                                   tm=PROMPT_TILE)
    ys, a_s, b_s, d_s, v_s = run_group(
        x_sample.reshape(1, nbs * dseq, D_MODEL), state_conv_a[:, None], state_conv_b[:, None],
        state_pool[:, None], nseg=nbs, seg_len=dseq, n_sub=1, start=PAST_LEN, tm=nbs * dseq)

    squeeze_p = lambda s: s.reshape(depth, nbp, s.shape[-2], D_BR)
    squeeze_s = lambda s: s.reshape(depth, nbs, s.shape[-2], D_BR)
    return (yp, ys.reshape(nbs, dseq, D_MODEL), squeeze_p(ap), squeeze_s(a_s), squeeze_p(bp), squeeze_s(b_s),
            squeeze_p(dp), squeeze_s(d_s), squeeze_p(vp), squeeze_s(v_s))
```

```python
import functools

import jax
import jax.numpy as jnp
import numpy as np
from jax import lax
from jax.experimental import pallas as pl
from jax.experimental.pallas import tpu as pltpu

D_MODEL = 2048
D_BR = D_MODEL // 4
D_FF = 5632
CONV_A_WIDTH = 31
CONV_B_WIDTH = 3
POOL_WINDOWS = (2, 4, 8, 16)
POOL_HIST = 15
POOL_GROUP_W = D_BR // 4
C_GROUPS = 4
C_GROUP_W = D_BR // C_GROUPS
SPATIAL_CHUNK = 128
PAST_LEN = 4096
RMS_EPS = 1e-6
LN_EPS = 1e-5

COL_A = (0, 2 * D_BR)
COL_B = (2 * D_BR, 5 * D_BR)
COL_C = (5 * D_BR, 7 * D_BR)
COL_D = (7 * D_BR, 8 * D_BR)
BR_COLS = 8 * D_BR

PAD_A = 32
PAD_B = 8
PAD_D = 16
SUBLANES = 8
ROW_BLOCK = 32

V7X_VMEM_LIMIT = 56 * 1024 * 1024

BF16 = jnp.bfloat16
F32 = jnp.float32


def _dot(a, b):
    return jnp.dot(a, b, preferred_element_type=F32)


def _rmsnorm(x, g):
    ms = jnp.mean(x * x, axis=-1, keepdims=True)
    return x * lax.rsqrt(ms + RMS_EPS) * g


def _layernorm(x, g, b):
    mu = jnp.mean(x, axis=-1, keepdims=True)
    xc = x - mu
    var = jnp.mean(xc * xc, axis=-1, keepdims=True)
    return xc * lax.rsqrt(var + LN_EPS) * g + b


def _silu(x):
    return x * jax.nn.sigmoid(x)


def _gelu_erf(x):
    return 0.5 * x * (1.0 + lax.erf(x * np.float32(np.sqrt(0.5))))


def _windowed_sum(ext_ref, seg, first_row, rows, taps, weights=None, cols=slice(None)):
    outs = []
    rb = min(ROW_BLOCK, rows)
    for r0 in range(0, rows, rb):
        acc = None
        for k in range(taps):
            term = ext_ref[seg, pl.ds(first_row + r0 + k, rb), cols]
            if weights is not None:
                term = term * weights[k]
            acc = term if acc is None else acc + term
        outs.append(acc)
    return outs[0] if len(outs) == 1 else jnp.concatenate(outs, axis=0)


def _conv_a(ea_ref, sh_ref, seg, weights, rows):
    ha = CONV_A_WIDTH - 1
    span = rows + PAD_A - SUBLANES
    for r in range(1, SUBLANES):
        sh_ref[r - 1, seg] = ea_ref[seg, pl.ds(r, span), :]
    outs = []
    rb = min(ROW_BLOCK, rows)
    for r0 in range(0, rows, rb):
        acc = None
        for k in range(CONV_A_WIDTH):
            first = PAD_A - ha + k
            r = first % SUBLANES
            if r == 0:
                term = ea_ref[seg, pl.ds(first + r0, rb), :]
            else:
                term = sh_ref[r - 1, seg, pl.ds(first - r + r0, rb), :]
            term = term * weights[k]
            acc = term if acc is None else acc + term
        outs.append(acc)
    return outs[0] if len(outs) == 1 else jnp.concatenate(outs, axis=0)


def _mix_kernel(x_ref, g_ref, w_ref, caw_ref, cab_ref, lnag_ref, lnab_ref, cbw_ref, lncg_ref, lncb_ref,
                wmix_ref, bmix_ref, poolw_ref, pscale_ref, ha_ref, hb_ref, hd_ref,
                h_ref, acts_ref, na_ref, nb_ref, nd_ref, vr_ref,
                ea_ref, eb_ref, ed_ref, sh_ref, *, nseg, seg_len, start, mix_chunk, vrows):
    i = pl.program_id(1)
    L = seg_len
    rows = nseg * L
    ha, hb, hd = CONV_A_WIDTH - 1, CONV_B_WIDTH - 1, POOL_HIST

    @pl.when(i == 0)
    def _load_history():
        ea_ref[:, PAD_A - ha:PAD_A, :] = ha_ref[...]
        eb_ref[:, PAD_B - hb:PAD_B, :] = hb_ref[...]
        ed_ref[:, PAD_D - hd:PAD_D, :] = hd_ref[...]

    caw = [caw_ref[k:k + 1, :] for k in range(CONV_A_WIDTH)]
    cbw = [cbw_ref[k:k + 1, :] for k in range(CONV_B_WIDTH)]
    segs = [(s, s * L) for s in range(nseg)]

    h = _rmsnorm(x_ref[...], g_ref[...]).astype(BF16)
    h_ref[...] = h

    za = _dot(h, w_ref[:, COL_A[0]:COL_A[1]])
    a = za[:, :D_BR] * jax.nn.sigmoid(za[:, D_BR:])
    for s, q0 in segs:
        ea_ref[s, PAD_A:PAD_A + L, :] = a[q0:q0 + L]
        conv = _conv_a(ea_ref, sh_ref, s, caw, L)
        a_act = _silu(_layernorm(conv + cab_ref[...], lnag_ref[...], lnab_ref[...]))
        acts_ref[q0:q0 + L, 0:D_BR] = a_act.astype(BF16)

    zb = _dot(h, w_ref[:, COL_B[0]:COL_B[1]])
    m = zb[:, D_BR:2 * D_BR] * zb[:, 2 * D_BR:]
    for s, q0 in segs:
        eb_ref[s, PAD_B:PAD_B + L, :] = m[q0:q0 + L]
        conv = _windowed_sum(eb_ref, s, PAD_B - hb, L, CONV_B_WIDTH, cbw)
        acts_ref[q0:q0 + L, D_BR:2 * D_BR] = (zb[q0:q0 + L, :D_BR] * conv).astype(BF16)

    zc = _gelu_erf(_dot(h, w_ref[:, COL_C[0]:COL_C[1]]))
    u = zc[:, :D_BR]
    v = _layernorm(zc[:, D_BR:], lncg_ref[...], lncb_ref[...])
    for s, q0 in segs:
        vr_ref[s] = v[q0 + L - vrows:q0 + L]
    vb = v.astype(BF16)
    for c0 in range(0, rows, mix_chunk):
        for g in range(C_GROUPS):
            cols = slice(g * C_GROUP_W, (g + 1) * C_GROUP_W)
            mixed = _dot(wmix_ref[g], vb[c0:c0 + mix_chunk, cols]) + bmix_ref[:, cols]
            acts_ref[c0:c0 + mix_chunk, 2 * D_BR + g * C_GROUP_W:2 * D_BR + (g + 1) * C_GROUP_W] = (
                u[c0:c0 + mix_chunk, cols] * mixed).astype(BF16)

    zd = _dot(h, w_ref[:, COL_D[0]:COL_D[1]])
    for s, q0 in segs:
        ed_ref[s, PAD_D:PAD_D + L, :] = zd[q0:q0 + L]
        pos = start + i * L + lax.broadcasted_iota(jnp.int32, (L, POOL_GROUP_W), 0)
        for g, wdw in enumerate(POOL_WINDOWS):
            cols = slice(g * POOL_GROUP_W, (g + 1) * POOL_GROUP_W)
            ssum = _windowed_sum(ed_ref, s, PAD_D - (wdw - 1), L, wdw, None, cols)
            cnt = jnp.minimum(pos + 1, wdw).astype(F32)
            pooled = ssum / cnt - zd[q0:q0 + L, cols]
            d_mix = _dot(pooled.astype(BF16), poolw_ref[g])
            acts_ref[q0:q0 + L, 3 * D_BR + g * POOL_GROUP_W:3 * D_BR + (g + 1) * POOL_GROUP_W] = (
                d_mix * pscale_ref[:, cols]).astype(BF16)

    for s in range(nseg):
        for ext_ref, new_ref, pad, hist in ((ea_ref, na_ref, PAD_A, ha), (eb_ref, nb_ref, PAD_B, hb),
                                            (ed_ref, nd_ref, PAD_D, hd)):
            tail = ext_ref[s, pad + L - hist:pad + L, :]
            new_ref[s] = tail
            ext_ref[s, pad - hist:pad, :] = tail


def _mix_call(x, hist_a, hist_b, hist_d, wts, wmix, bmix, *, layer, nseg, seg_len, start):
    nb, t, _ = x.shape
    tm = nseg * seg_len
    n_tiles = t // tm
    assert t % tm == 0 and (nseg == 1 or n_tiles == 1)
    mix_chunk = wmix.shape[-1]
    assert tm % mix_chunk == 0
    vrows = min(seg_len, SPATIAL_CHUNK)

    def of_layer(arr):
        return pl.BlockSpec((None,) + arr.shape[1:], lambda b, i, _n=arr.ndim: (layer,) + (0,) * (_n - 1))

    def per_batch(arr):
        return pl.BlockSpec((None,) + arr.shape[1:], lambda b, i, _n=arr.ndim: (b,) + (0,) * (_n - 1))

    tile = pl.BlockSpec((None, tm, D_MODEL), lambda b, i: (b, i, 0))
    w_br_spec = pl.BlockSpec((None, D_MODEL, BR_COLS), lambda b, i: (layer, 0, 0),
                             pipeline_mode=pl.Buffered(1))
    small = [wts['conv_a_w'], wts['conv_a_b'], wts['ln_a_g'], wts['ln_a_b'], wts['conv_b_w'], wts['ln_c_g'],
             wts['ln_c_b'], wmix, bmix, wts['pool_w'], wts['pool_scale']]
    out_shape = (
        jax.ShapeDtypeStruct((nb, t, D_MODEL), BF16),
        jax.ShapeDtypeStruct((nb, t, D_MODEL), BF16),
        jax.ShapeDtypeStruct((nb, nseg, CONV_A_WIDTH - 1, D_BR), F32),
        jax.ShapeDtypeStruct((nb, nseg, CONV_B_WIDTH - 1, D_BR), F32),
        jax.ShapeDtypeStruct((nb, nseg, POOL_HIST, D_BR), F32),
        jax.ShapeDtypeStruct((nb, nseg, vrows, D_BR), F32),
    )
    kern = functools.partial(_mix_kernel, nseg=nseg, seg_len=seg_len, start=start, mix_chunk=mix_chunk,
                             vrows=vrows)
    return pl.pallas_call(
        kern,
        grid=(nb, n_tiles),
        in_specs=[tile, of_layer(wts['norm_mix_g']), w_br_spec] + [of_layer(a) for a in small]
        + [per_batch(hist_a), per_batch(hist_b), per_batch(hist_d)],
        out_specs=(tile, tile) + tuple(per_batch(s) for s in out_shape[2:]),
        out_shape=out_shape,
        scratch_shapes=[
            pltpu.VMEM((nseg, PAD_A + seg_len, D_BR), F32),
            pltpu.VMEM((nseg, PAD_B + seg_len, D_BR), F32),
            pltpu.VMEM((nseg, PAD_D + seg_len, D_BR), F32),
            pltpu.VMEM((SUBLANES - 1, nseg, PAD_A - SUBLANES + seg_len, D_BR), F32),
        ],
        compiler_params=pltpu.CompilerParams(dimension_semantics=("arbitrary", "arbitrary"),
                                             vmem_limit_bytes=V7X_VMEM_LIMIT),
        name="mix",
    )(x, wts['norm_mix_g'], wts['w_in'], *small, hist_a, hist_b, hist_d)


def _gate_kernel(h_ref, acts_ref, wg0, wg1, wg2, wg3, wo0, wo1, wo2, wo3, out_ref):
    h = h_ref[...]
    acc = None
    for b, (wg, wo) in enumerate(((wg0, wo0), (wg1, wo1), (wg2, wo2), (wg3, wo3))):
        gate = jax.nn.sigmoid(_dot(h, wg[...]))
        term = gate * _dot(acts_ref[:, b * D_BR:(b + 1) * D_BR], wo[...])
        acc = term if acc is None else acc + term
    out_ref[...] = acc.astype(BF16)


def _gate_call(h, acts, wts, *, layer, tm, nc):
    n = h.shape[0]
    n_col = D_MODEL // nc
    gate0 = BR_COLS // nc
    act_tile = pl.BlockSpec((tm, D_MODEL), lambda i, j: (i, 0))
    wg_specs = [pl.BlockSpec((None, D_MODEL, nc), lambda i, j, _b=b: (layer, 0, gate0 + _b * n_col + j))
                for b in range(4)]
    wo_spec = pl.BlockSpec((None, D_BR, nc), lambda i, j: (layer, 0, j))
    return pl.pallas_call(
        _gate_kernel,
        grid=(n // tm, n_col),
        in_specs=[act_tile, act_tile] + wg_specs + [wo_spec] * 4,
        out_specs=pl.BlockSpec((tm, nc), lambda i, j: (i, j)),
        out_shape=jax.ShapeDtypeStruct((n, D_MODEL), BF16),
        compiler_params=pltpu.CompilerParams(dimension_semantics=("parallel", "arbitrary"),
                                             vmem_limit_bytes=V7X_VMEM_LIMIT),
        name="gate",
    )(h, acts, wts['w_in'], wts['w_in'], wts['w_in'], wts['w_in'],
      wts['w_out_a'], wts['w_out_b'], wts['w_out_c'], wts['w_out_d'])


def _oproj_kernel(x_ref, m_ref, wo_ref, out_ref):
    out_ref[...] = x_ref[...] + _dot(m_ref[...], wo_ref[...])


def _oproj_call(x, merged, wts, *, layer, tm):
    n = x.shape[0]
    return pl.pallas_call(
        _oproj_kernel,
        grid=(n // tm,),
        in_specs=[pl.BlockSpec((tm, D_MODEL), lambda i: (i, 0)),
                  pl.BlockSpec((tm, D_MODEL), lambda i: (i, 0)),
                  pl.BlockSpec((None, D_MODEL, D_MODEL), lambda i: (layer, 0, 0),
                               pipeline_mode=pl.Buffered(1))],
        out_specs=pl.BlockSpec((tm, D_MODEL), lambda i: (i, 0)),
        out_shape=jax.ShapeDtypeStruct((n, D_MODEL), F32),
        compiler_params=pltpu.CompilerParams(dimension_semantics=("parallel",),
                                             vmem_limit_bytes=V7X_VMEM_LIMIT),
        name="oproj",
    )(x, merged, wts['w_o'])


def _ffn_kernel(x_ref, g_ref, w13_ref, w2_ref, gfin_ref, out_ref, h2_ref, *, final_norm):
    j = pl.program_id(1)
    fc = w2_ref.shape[0]

    @pl.when(j == 0)
    def _start():
        x = x_ref[...]
        h2_ref[...] = _rmsnorm(x, g_ref[...]).astype(BF16)
        out_ref[...] = x

    up = _dot(h2_ref[...], w13_ref[...])
    hidden = (_silu(up[:, :fc]) * up[:, fc:]).astype(BF16)
    out_ref[...] += _dot(hidden, w2_ref[...])

    if final_norm:
        @pl.when(j == pl.num_programs(1) - 1)
        def _finish():
            out_ref[...] = _rmsnorm(out_ref[...], gfin_ref[...])


def _ffn_call(x, wts, norm_final_g, *, layer, tm, fc, final_norm):
    n = x.shape[0]
    kern = functools.partial(_ffn_kernel, final_norm=final_norm)
    return pl.pallas_call(
        kern,
        grid=(n // tm, D_FF // fc),
        in_specs=[pl.BlockSpec((tm, D_MODEL), lambda i, j: (i, 0)),
                  pl.BlockSpec((None, 1, D_MODEL), lambda i, j: (layer, 0, 0)),
                  pl.BlockSpec((None, D_MODEL, 2 * fc), lambda i, j: (layer, 0, j)),
                  pl.BlockSpec((None, fc, D_MODEL), lambda i, j: (layer, j, 0)),
                  pl.BlockSpec((1, D_MODEL), lambda i, j: (0, 0))],
        out_specs=pl.BlockSpec((tm, D_MODEL), lambda i, j: (i, 0)),
        out_shape=jax.ShapeDtypeStruct((n, D_MODEL), F32),
        scratch_shapes=[pltpu.VMEM((tm, D_MODEL), BF16)],
        compiler_params=pltpu.CompilerParams(dimension_semantics=("parallel", "arbitrary"),
                                             vmem_limit_bytes=V7X_VMEM_LIMIT),
        name="ffn",
    )(x, wts['norm_ffn_g'], wts['ffn_w13'], wts['ffn_w2'], norm_final_g)


def _interleave_column_blocks(parts, block):
    depth, k, n = parts[0].shape
    stacked = jnp.stack([p.reshape(depth, k, n // block, block) for p in parts], axis=3)
    return stacked.reshape(depth, k, len(parts) * n)


def _spatial_mix_operands(spatial_w, spatial_b, seg_len):
    depth = spatial_w.shape[0]
    tri = jnp.tril(jnp.ones((SPATIAL_CHUNK, SPATIAL_CHUNK), dtype=bool))
    wm = jnp.where(tri, spatial_w, jnp.zeros((), spatial_w.dtype))
    bias = jnp.repeat(jnp.swapaxes(spatial_b, 1, 2), C_GROUP_W, axis=2)
    if seg_len >= SPATIAL_CHUNK:
        assert seg_len % SPATIAL_CHUNK == 0
        return wm.astype(BF16), bias
    reps = 2 * SPATIAL_CHUNK // seg_len
    eye = jnp.eye(reps, dtype=wm.dtype)
    corner = wm[:, :, :seg_len, :seg_len]
    blockdiag = jnp.einsum('pq,lgts->lgptqs', eye, corner).reshape(
        depth, C_GROUPS, reps * seg_len, reps * seg_len)
    return blockdiag.astype(BF16), jnp.tile(bias[:, :seg_len], (1, reps, 1))


PROMPT_MIX_TILE = 512
PROMPT_TILE = 1024
GATE_COLS = 512
FFN_COLS = 512


def kernel(x_prompt, x_sample, state_conv_a, state_conv_b, state_pool, norm_mix_g, w_in, conv_a_w, conv_a_b,
           ln_a_g, ln_a_b, w_out_a, conv_b_w, w_out_b, ln_c_g, ln_c_b, spatial_w, spatial_b, w_out_c, pool_w,
           pool_scale, w_out_d, w_o, norm_ffn_g, ffn_w1, ffn_w3, ffn_w2, norm_final_g):
    depth = w_in.shape[0]
    nbp, seq, _ = x_prompt.shape
    nbs, dseq, _ = x_sample.shape

    rows = lambda v: v.reshape(depth, 1, -1)
    wts = dict(
        w_in=w_in.astype(BF16),
        w_out_a=w_out_a.astype(BF16), w_out_b=w_out_b.astype(BF16),
        w_out_c=w_out_c.astype(BF16), w_out_d=w_out_d.astype(BF16),
        w_o=w_o.astype(BF16), pool_w=pool_w.astype(BF16),
        ffn_w13=_interleave_column_blocks([ffn_w1, ffn_w3], FFN_COLS).astype(BF16),
        ffn_w2=ffn_w2.astype(BF16),
        norm_mix_g=rows(norm_mix_g), norm_ffn_g=rows(norm_ffn_g),
        conv_a_w=conv_a_w, conv_a_b=rows(conv_a_b), ln_a_g=rows(ln_a_g), ln_a_b=rows(ln_a_b),
        conv_b_w=conv_b_w, ln_c_g=rows(ln_c_g), ln_c_b=rows(ln_c_b), pool_scale=rows(pool_scale),
    )
    gfin = norm_final_g.reshape(1, -1)

    def run_group(x, hist_a, hist_b, hist_d, *, nseg, seg_len, start, tm):
        nb, t, _ = x.shape
        wmix, bmix = _spatial_mix_operands(spatial_w, spatial_b, seg_len)
        states = []
        for l in range(depth):
            h, acts, na, nbuf, nd, vr = _mix_call(x, hist_a[l], hist_b[l], hist_d[l], wts, wmix, bmix,
                                                  layer=l, nseg=nseg, seg_len=seg_len, start=start)
            xf = x.reshape(nb * t, D_MODEL)
            merged = _gate_call(h.reshape(nb * t, D_MODEL), acts.reshape(nb * t, D_MODEL), wts,
                                layer=l, tm=tm, nc=GATE_COLS)
            x1 = _oproj_call(xf, merged, wts, layer=l, tm=min(tm, 512))
            x = _ffn_call(x1, wts, gfin, layer=l, tm=tm, fc=FFN_COLS,
                          final_norm=(l == depth - 1)).reshape(nb, t, D_MODEL)
            states.append((na, nbuf, nd, vr))
        return (x,) + tuple(jnp.stack(s) for s in zip(*states))

    dt = x_prompt.dtype
    zeros = lambda rows: jnp.zeros((depth, nbp, 1, rows, D_BR), dt)
    yp, ap, bp, dp, vp = run_group(x_prompt, zeros(CONV_A_WIDTH - 1), zeros(CONV_B_WIDTH - 1), zeros(POOL_HIST),
                                   nseg=1, seg_len=PROMPT_MIX_TILE, start=0, tm=PROMPT_TILE)
    ys, a_s, b_s, d_s, v_s = run_group(
        x_sample.reshape(1, nbs * dseq, D_MODEL), state_conv_a[:, None], state_conv_b[:, None],
        state_pool[:, None], nseg=nbs, seg_len=dseq, start=PAST_LEN, tm=nbs * dseq)

    squeeze_p = lambda s: s.reshape(depth, nbp, s.shape[-2], D_BR)
    squeeze_s = lambda s: s.reshape(depth, nbs, s.shape[-2], D_BR)
    return (yp, ys.reshape(nbs, dseq, D_MODEL), squeeze_p(ap), squeeze_s(a_s), squeeze_p(bp), squeeze_s(b_s),
            squeeze_p(dp), squeeze_s(d_s), squeeze_p(vp), squeeze_s(v_s))
```

```python
import functools

import jax
import jax.numpy as jnp
import numpy as np
from jax import lax
from jax.experimental import pallas as pl
from jax.experimental.pallas import tpu as pltpu

D_MODEL = 2048
D_BR = D_MODEL // 4
D_FF = 5632
CONV_A_WIDTH = 31
CONV_B_WIDTH = 3
POOL_WINDOWS = (2, 4, 8, 16)
POOL_HIST = 15
POOL_GROUP_W = D_BR // 4
C_GROUPS = 4
C_GROUP_W = D_BR // C_GROUPS
SPATIAL_CHUNK = 128
PAST_LEN = 4096
RMS_EPS = 1e-6
LN_EPS = 1e-5

COL_A = (0, 2 * D_BR)
COL_B = (2 * D_BR, 5 * D_BR)
COL_C = (5 * D_BR, 7 * D_BR)
COL_D = (7 * D_BR, 8 * D_BR)
BR_COLS = 8 * D_BR

PAD_A = 32
PAD_B = 8
PAD_D = 24
SUBLANES = 8
ROW_BLOCK = 32

V7X_VMEM_LIMIT = 56 * 1024 * 1024

BF16 = jnp.bfloat16
F32 = jnp.float32


def _dot(a, b):
    return jnp.dot(a, b, preferred_element_type=F32)


def _rmsnorm(x, g):
    ms = jnp.mean(x * x, axis=-1, keepdims=True)
    return x * lax.rsqrt(ms + RMS_EPS) * g


def _layernorm(x, g, b):
    mu = jnp.mean(x, axis=-1, keepdims=True)
    xc = x - mu
    var = jnp.mean(xc * xc, axis=-1, keepdims=True)
    return xc * lax.rsqrt(var + LN_EPS) * g + b


def _sigmoid(x):
    return 0.5 * (jnp.tanh(0.5 * x) + 1.0)


def _silu(x):
    return x * _sigmoid(x)


def _gelu_erf(x):
    return 0.5 * x * (1.0 + lax.erf(x * np.float32(np.sqrt(0.5))))


def _windowed_sum(ext_ref, seg, first_row, rows, taps, weights=None, cols=slice(None), rb=ROW_BLOCK):
    outs = []
    rb = min(rb, rows)
    for r0 in range(0, rows, rb):
        acc = None
        for k in range(taps):
            term = ext_ref[seg, pl.ds(first_row + r0 + k, rb), cols]
            if weights is not None:
                term = term * weights[k]
            acc = term if acc is None else acc + term
        outs.append(acc)
    return outs[0] if len(outs) == 1 else jnp.concatenate(outs, axis=0)


def _conv_a(ea_ref, sh_ref, seg, weights, rows):
    ha = CONV_A_WIDTH - 1
    span = rows + PAD_A - SUBLANES
    for r in range(1, SUBLANES):
        sh_ref[r - 1, seg] = ea_ref[seg, pl.ds(r, span), :]
    outs = []
    rb = min(ROW_BLOCK, rows)
    for r0 in range(0, rows, rb):
        acc = None
        for k in range(CONV_A_WIDTH):
            first = PAD_A - ha + k
            r = first % SUBLANES
            if r == 0:
                term = ea_ref[seg, pl.ds(first + r0, rb), :]
            else:
                term = sh_ref[r - 1, seg, pl.ds(first - r + r0, rb), :]
            term = term * weights[k]
            acc = term if acc is None else acc + term
        outs.append(acc)
    return outs[0] if len(outs) == 1 else jnp.concatenate(outs, axis=0)


def _mix_kernel(x_ref, g_ref, w_ref, caw_ref, cab_ref, lnag_ref, lnab_ref, cbw_ref, lncg_ref, lncb_ref,
                wmix_ref, bmix_ref, poolw_ref, pscale_ref, ha_ref, hb_ref, hd_ref,
                h_ref, acts_ref, na_ref, nb_ref, nd_ref, vr_ref,
                ea_ref, eb_ref, ed_ref, sh_ref, pw_ref, *, nseg, seg_len, start, mix_chunk, vrows):
    i = pl.program_id(1)
    L = seg_len
    rows = nseg * L
    ha, hb, hd = CONV_A_WIDTH - 1, CONV_B_WIDTH - 1, POOL_HIST

    @pl.when(i == 0)
    def _load_history():
        ea_ref[:, 0:PAD_A - ha, :] = jnp.zeros((nseg, PAD_A - ha, D_BR), F32)
        ea_ref[:, PAD_A - ha:PAD_A, :] = ha_ref[...]
        eb_ref[:, PAD_B - hb:PAD_B, :] = hb_ref[...]
        ed_ref[:, 0:PAD_D - hd, :] = jnp.zeros((nseg, PAD_D - hd, D_BR), F32)
        ed_ref[:, PAD_D - hd:PAD_D, :] = hd_ref[...]
        pw_ref[:, :, 0:SUBLANES, :] = jnp.zeros((2, nseg, SUBLANES, D_BR), F32)

    caw = [caw_ref[k:k + 1, :] for k in range(CONV_A_WIDTH)]
    cbw = [cbw_ref[k:k + 1, :] for k in range(CONV_B_WIDTH)]
    segs = [(s, s * L) for s in range(nseg)]

    h = _rmsnorm(x_ref[...], g_ref[...]).astype(BF16)
    h_ref[...] = h

    za = _dot(h, w_ref[:, COL_A[0]:COL_A[1]])
    a = za[:, :D_BR] * _sigmoid(za[:, D_BR:])
    for s, q0 in segs:
        ea_ref[s, PAD_A:PAD_A + L, :] = a[q0:q0 + L]
        conv = _conv_a(ea_ref, sh_ref, s, caw, L)
        a_act = _silu(_layernorm(conv + cab_ref[...], lnag_ref[...], lnab_ref[...]))
        acts_ref[q0:q0 + L, 0:D_BR] = a_act.astype(BF16)

    zb = _dot(h, w_ref[:, COL_B[0]:COL_B[1]])
    m = zb[:, D_BR:2 * D_BR] * zb[:, 2 * D_BR:]
    for s, q0 in segs:
        eb_ref[s, PAD_B:PAD_B + L, :] = m[q0:q0 + L]
        conv = _windowed_sum(eb_ref, s, PAD_B - hb, L, CONV_B_WIDTH, cbw)
        acts_ref[q0:q0 + L, D_BR:2 * D_BR] = (zb[q0:q0 + L, :D_BR] * conv).astype(BF16)

    zc = _gelu_erf(_dot(h, w_ref[:, COL_C[0]:COL_C[1]]))
    u = zc[:, :D_BR]
    v = _layernorm(zc[:, D_BR:], lncg_ref[...], lncb_ref[...])
    for s, q0 in segs:
        vr_ref[s] = v[q0 + L - vrows:q0 + L]
    vb = v.astype(BF16)
    for c0 in range(0, rows, mix_chunk):
        for g in range(C_GROUPS):
            cols = slice(g * C_GROUP_W, (g + 1) * C_GROUP_W)
            mixed = _dot(wmix_ref[g], vb[c0:c0 + mix_chunk, cols]) + bmix_ref[:, cols]
            acts_ref[c0:c0 + mix_chunk, 2 * D_BR + g * C_GROUP_W:2 * D_BR + (g + 1) * C_GROUP_W] = (
                u[c0:c0 + mix_chunk, cols] * mixed).astype(BF16)

    zd = _dot(h, w_ref[:, COL_D[0]:COL_D[1]])
    for s, q0 in segs:
        ed_ref[s, PAD_D:PAD_D + L, :] = zd[q0:q0 + L]
        pos = start + i * L + lax.broadcasted_iota(jnp.int32, (L, POOL_GROUP_W), 0)
        span = PAD_D + L - SUBLANES
        g1, g2, g3 = (slice(g * POOL_GROUP_W, D_BR) for g in (1, 2, 3))
        s2 = _windowed_sum(ed_ref, s, SUBLANES - 1, span, 2, rb=PAD_D - SUBLANES)
        pw_ref[0, s, SUBLANES:, g1] = s2[:, g1]
        s4 = pw_ref[0, s, SUBLANES:, g1] + pw_ref[0, s, pl.ds(SUBLANES - 2, span), g1]
        pw_ref[1, s, SUBLANES:, g2] = s4[:, POOL_GROUP_W:]
        s8 = pw_ref[1, s, SUBLANES:, g2] + pw_ref[1, s, pl.ds(SUBLANES - 4, span), g2]
        pw_ref[0, s, SUBLANES:, g3] = s8[:, POOL_GROUP_W:]
        s16 = pw_ref[0, s, PAD_D:, g3] + pw_ref[0, s, PAD_D - SUBLANES:PAD_D - SUBLANES + L, g3]
        t0 = PAD_D - SUBLANES
        sums = (s2[t0:, :POOL_GROUP_W], s4[t0:, :POOL_GROUP_W], s8[t0:, :POOL_GROUP_W], s16)
        for g, wdw in enumerate(POOL_WINDOWS):
            cols = slice(g * POOL_GROUP_W, (g + 1) * POOL_GROUP_W)
            ssum = sums[g]
            cnt = jnp.minimum(pos + 1, wdw).astype(F32)
            pooled = ssum / cnt - zd[q0:q0 + L, cols]
            d_mix = _dot(pooled.astype(BF16), poolw_ref[g])
            acts_ref[q0:q0 + L, 3 * D_BR + g * POOL_GROUP_W:3 * D_BR + (g + 1) * POOL_GROUP_W] = (
                d_mix * pscale_ref[:, cols]).astype(BF16)

    for s in range(nseg):
        for ext_ref, new_ref, pad, hist in ((ea_ref, na_ref, PAD_A, ha), (eb_ref, nb_ref, PAD_B, hb),
                                            (ed_ref, nd_ref, PAD_D, hd)):
            tail = ext_ref[s, pad + L - hist:pad + L, :]
            new_ref[s] = tail
            ext_ref[s, pad - hist:pad, :] = tail


def _mix_call(x, hist_a, hist_b, hist_d, wts, wmix, bmix, *, layer, nseg, seg_len, start):
    nb, t, _ = x.shape
    tm = nseg * seg_len
    n_tiles = t // tm
    assert t % tm == 0 and (nseg == 1 or n_tiles == 1)
    mix_chunk = wmix.shape[-1]
    assert tm % mix_chunk == 0
    vrows = min(seg_len, SPATIAL_CHUNK)

    def of_layer(arr):
        return pl.BlockSpec((None,) + arr.shape[1:], lambda b, i, _n=arr.ndim: (layer,) + (0,) * (_n - 1))

    def per_batch(arr):
        return pl.BlockSpec((None,) + arr.shape[1:], lambda b, i, _n=arr.ndim: (b,) + (0,) * (_n - 1))

    tile = pl.BlockSpec((None, tm, D_MODEL), lambda b, i: (b, i, 0))
    w_br_spec = pl.BlockSpec((None, D_MODEL, BR_COLS), lambda b, i: (layer, 0, 0),
                             pipeline_mode=pl.Buffered(1))
    small = [wts['conv_a_w'], wts['conv_a_b'], wts['ln_a_g'], wts['ln_a_b'], wts['conv_b_w'], wts['ln_c_g'],
             wts['ln_c_b'], wmix, bmix, wts['pool_w'], wts['pool_scale']]
    out_shape = (
        jax.ShapeDtypeStruct((nb, t, D_MODEL), BF16),
        jax.ShapeDtypeStruct((nb, t, D_MODEL), BF16),
        jax.ShapeDtypeStruct((nb, nseg, CONV_A_WIDTH - 1, D_BR), F32),
        jax.ShapeDtypeStruct((nb, nseg, CONV_B_WIDTH - 1, D_BR), F32),
        jax.ShapeDtypeStruct((nb, nseg, POOL_HIST, D_BR), F32),
        jax.ShapeDtypeStruct((nb, nseg, vrows, D_BR), F32),
    )
    kern = functools.partial(_mix_kernel, nseg=nseg, seg_len=seg_len, start=start, mix_chunk=mix_chunk,
                             vrows=vrows)
    return pl.pallas_call(
        kern,
        grid=(nb, n_tiles),
        in_specs=[tile, of_layer(wts['norm_mix_g']), w_br_spec] + [of_layer(a) for a in small]
        + [per_batch(hist_a), per_batch(hist_b), per_batch(hist_d)],
        out_specs=(tile, tile) + tuple(per_batch(s) for s in out_shape[2:]),
        out_shape=out_shape,
        scratch_shapes=[
            pltpu.VMEM((nseg, PAD_A + seg_len, D_BR), F32),
            pltpu.VMEM((nseg, PAD_B + seg_len, D_BR), F32),
            pltpu.VMEM((nseg, PAD_D + seg_len, D_BR), F32),
            pltpu.VMEM((SUBLANES - 1, nseg, PAD_A - SUBLANES + seg_len, D_BR), F32),
            pltpu.VMEM((2, nseg, PAD_D + seg_len, D_BR), F32),
        ],
        compiler_params=pltpu.CompilerParams(dimension_semantics=("arbitrary", "arbitrary"),
                                             vmem_limit_bytes=V7X_VMEM_LIMIT),
        name="mix",
    )(x, wts['norm_mix_g'], wts['w_in'], *small, hist_a, hist_b, hist_d)


def _gate_kernel(h_ref, acts_ref, wg0, wg1, wg2, wg3, wo0, wo1, wo2, wo3, out_ref):
    h = h_ref[...]
    acc = None
    for b, (wg, wo) in enumerate(((wg0, wo0), (wg1, wo1), (wg2, wo2), (wg3, wo3))):
        gate = _sigmoid(_dot(h, wg[...]))
        term = gate * _dot(acts_ref[:, b * D_BR:(b + 1) * D_BR], wo[...])
        acc = term if acc is None else acc + term
    out_ref[...] = acc.astype(BF16)


def _gate_call(h, acts, wts, *, layer, tm, nc):
    n = h.shape[0]
    n_col = D_MODEL // nc
    gate0 = BR_COLS // nc
    act_tile = pl.BlockSpec((tm, D_MODEL), lambda i, j: (i, 0))
    wg_specs = [pl.BlockSpec((None, D_MODEL, nc), lambda i, j, _b=b: (layer, 0, gate0 + _b * n_col + j))
                for b in range(4)]
    wo_spec = pl.BlockSpec((None, D_BR, nc), lambda i, j: (layer, 0, j))
    return pl.pallas_call(
        _gate_kernel,
        grid=(n // tm, n_col),
        in_specs=[act_tile, act_tile] + wg_specs + [wo_spec] * 4,
        out_specs=pl.BlockSpec((tm, nc), lambda i, j: (i, j)),
        out_shape=jax.ShapeDtypeStruct((n, D_MODEL), BF16),
        compiler_params=pltpu.CompilerParams(dimension_semantics=("parallel", "arbitrary"),
                                             vmem_limit_bytes=V7X_VMEM_LIMIT),
        name="gate",
    )(h, acts, wts['w_in'], wts['w_in'], wts['w_in'], wts['w_in'],
      wts['w_out_a'], wts['w_out_b'], wts['w_out_c'], wts['w_out_d'])


def _oproj_kernel(x_ref, m_ref, wo_ref, out_ref):
    out_ref[...] = x_ref[...] + _dot(m_ref[...], wo_ref[...])


def _oproj_call(x, merged, wts, *, layer, tm):
    n = x.shape[0]
    return pl.pallas_call(
        _oproj_kernel,
        grid=(n // tm,),
        in_specs=[pl.BlockSpec((tm, D_MODEL), lambda i: (i, 0)),
                  pl.BlockSpec((tm, D_MODEL), lambda i: (i, 0)),
                  pl.BlockSpec((None, D_MODEL, D_MODEL), lambda i: (layer, 0, 0),
                               pipeline_mode=pl.Buffered(1))],
        out_specs=pl.BlockSpec((tm, D_MODEL), lambda i: (i, 0)),
        out_shape=jax.ShapeDtypeStruct((n, D_MODEL), F32),
        compiler_params=pltpu.CompilerParams(dimension_semantics=("parallel",),
                                             vmem_limit_bytes=V7X_VMEM_LIMIT),
        name="oproj",
    )(x, merged, wts['w_o'])


def _ffn_kernel(x_ref, g_ref, w1_ref, w3_ref, w2_ref, gfin_ref, out_ref, h2_ref, *, final_norm):
    j = pl.program_id(1)

    @pl.when(j == 0)
    def _start():
        x = x_ref[...]
        h2_ref[...] = _rmsnorm(x, g_ref[...]).astype(BF16)
        out_ref[...] = x

    h2 = h2_ref[...]
    hidden = (_silu(_dot(h2, w1_ref[...])) * _dot(h2, w3_ref[...])).astype(BF16)
    out_ref[...] += _dot(hidden, w2_ref[...])

    if final_norm:
        @pl.when(j == pl.num_programs(1) - 1)
        def _finish():
            out_ref[...] = _rmsnorm(out_ref[...], gfin_ref[...])


def _ffn_call(x, wts, norm_final_g, *, layer, tm, fc, final_norm):
    n = x.shape[0]
    kern = functools.partial(_ffn_kernel, final_norm=final_norm)
    return pl.pallas_call(
        kern,
        grid=(n // tm, D_FF // fc),
        in_specs=[pl.BlockSpec((tm, D_MODEL), lambda i, j: (i, 0)),
                  pl.BlockSpec((None, 1, D_MODEL), lambda i, j: (layer, 0, 0)),
                  pl.BlockSpec((None, D_MODEL, fc), lambda i, j: (layer, 0, j)),
                  pl.BlockSpec((None, D_MODEL, fc), lambda i, j: (layer, 0, j)),
                  pl.BlockSpec((None, fc, D_MODEL), lambda i, j: (layer, j, 0)),
                  pl.BlockSpec((1, D_MODEL), lambda i, j: (0, 0))],
        out_specs=pl.BlockSpec((tm, D_MODEL), lambda i, j: (i, 0)),
        out_shape=jax.ShapeDtypeStruct((n, D_MODEL), F32),
        scratch_shapes=[pltpu.VMEM((tm, D_MODEL), BF16)],
        compiler_params=pltpu.CompilerParams(dimension_semantics=("parallel", "arbitrary"),
                                             vmem_limit_bytes=V7X_VMEM_LIMIT),
        name="ffn",
    )(x, wts['norm_ffn_g'], wts['ffn_w1'], wts['ffn_w3'], wts['ffn_w2'], norm_final_g)


def _spatial_mix_operands(spatial_w, spatial_b, seg_len):
    depth = spatial_w.shape[0]
    tri = jnp.tril(jnp.ones((SPATIAL_CHUNK, SPATIAL_CHUNK), dtype=bool))
    wm = jnp.where(tri, spatial_w, jnp.zeros((), spatial_w.dtype))
    bias = jnp.repeat(jnp.swapaxes(spatial_b, 1, 2), C_GROUP_W, axis=2)
    if seg_len >= SPATIAL_CHUNK:
        assert seg_len % SPATIAL_CHUNK == 0
        return wm.astype(BF16), bias
    reps = 2 * SPATIAL_CHUNK // seg_len
    eye = jnp.eye(reps, dtype=wm.dtype)
    corner = wm[:, :, :seg_len, :seg_len]
    blockdiag = jnp.einsum('pq,lgts->lgptqs', eye, corner).reshape(
        depth, C_GROUPS, reps * seg_len, reps * seg_len)
    return blockdiag.astype(BF16), jnp.tile(bias[:, :seg_len], (1, reps, 1))


PROMPT_MIX_TILE = 512
PROMPT_TILE = 1024
GATE_COLS = 512
FFN_COLS = 512


def kernel(x_prompt, x_sample, state_conv_a, state_conv_b, state_pool, norm_mix_g, w_in, conv_a_w, conv_a_b,
           ln_a_g, ln_a_b, w_out_a, conv_b_w, w_out_b, ln_c_g, ln_c_b, spatial_w, spatial_b, w_out_c, pool_w,
           pool_scale, w_out_d, w_o, norm_ffn_g, ffn_w1, ffn_w3, ffn_w2, norm_final_g):
    depth = w_in.shape[0]
    nbp, seq, _ = x_prompt.shape
    nbs, dseq, _ = x_sample.shape

    rows = lambda v: v.reshape(depth, 1, -1)
    wts = dict(
        w_in=w_in.astype(BF16),
        w_out_a=w_out_a.astype(BF16), w_out_b=w_out_b.astype(BF16),
        w_out_c=w_out_c.astype(BF16), w_out_d=w_out_d.astype(BF16),
        w_o=w_o.astype(BF16), pool_w=pool_w.astype(BF16),
        ffn_w1=ffn_w1.astype(BF16), ffn_w3=ffn_w3.astype(BF16), ffn_w2=ffn_w2.astype(BF16),
        norm_mix_g=rows(norm_mix_g), norm_ffn_g=rows(norm_ffn_g),
        conv_a_w=conv_a_w, conv_a_b=rows(conv_a_b), ln_a_g=rows(ln_a_g), ln_a_b=rows(ln_a_b),
        conv_b_w=conv_b_w, ln_c_g=rows(ln_c_g), ln_c_b=rows(ln_c_b), pool_scale=rows(pool_scale),
    )
    gfin = norm_final_g.reshape(1, -1)

    def run_group(x, hist_a, hist_b, hist_d, *, nseg, seg_len, start, tm):
        nb, t, _ = x.shape
        wmix, bmix = _spatial_mix_operands(spatial_w, spatial_b, seg_len)
        states = []
        for l in range(depth):
            h, acts, na, nbuf, nd, vr = _mix_call(x, hist_a[l], hist_b[l], hist_d[l], wts, wmix, bmix,
                                                  layer=l, nseg=nseg, seg_len=seg_len, start=start)
            xf = x.reshape(nb * t, D_MODEL)
            merged = _gate_call(h.reshape(nb * t, D_MODEL), acts.reshape(nb * t, D_MODEL), wts,
                                layer=l, tm=tm, nc=GATE_COLS)
            x1 = _oproj_call(xf, merged, wts, layer=l, tm=tm)
            x = _ffn_call(x1, wts, gfin, layer=l, tm=tm, fc=FFN_COLS,
                          final_norm=(l == depth - 1)).reshape(nb, t, D_MODEL)
            states.append((na, nbuf, nd, vr))
        return (x,) + tuple(jnp.stack(s) for s in zip(*states))

    dt = x_prompt.dtype
    zeros = lambda rows: jnp.zeros((depth, nbp, 1, rows, D_BR), dt)
    yp, ap, bp, dp, vp = run_group(x_prompt, zeros(CONV_A_WIDTH - 1), zeros(CONV_B_WIDTH - 1), zeros(POOL_HIST),
                                   nseg=1, seg_len=PROMPT_MIX_TILE, start=0, tm=PROMPT_TILE)
    ys, a_s, b_s, d_s, v_s = run_group(
        x_sample.reshape(1, nbs * dseq, D_MODEL), state_conv_a[:, None], state_conv_b[:, None],
        state_pool[:, None], nseg=nbs, seg_len=dseq, start=PAST_LEN, tm=nbs * dseq)

    squeeze_p = lambda s: s.reshape(depth, nbp, s.shape[-2], D_BR)
    squeeze_s = lambda s: s.reshape(depth, nbs, s.shape[-2], D_BR)
    return (yp, ys.reshape(nbs, dseq, D_MODEL), squeeze_p(ap), squeeze_s(a_s), squeeze_p(bp), squeeze_s(b_s),
            squeeze_p(dp), squeeze_s(d_s), squeeze_p(vp), squeeze_s(v_s))
```

```python
import functools

import jax
import jax.numpy as jnp
import numpy as np
from jax import lax
from jax.experimental import pallas as pl
from jax.experimental.pallas import tpu as pltpu

D_MODEL = 2048
D_BR = D_MODEL // 4
D_FF = 5632
CONV_A_WIDTH = 31
CONV_B_WIDTH = 3
POOL_WINDOWS = (2, 4, 8, 16)
POOL_HIST = 15
POOL_GROUP_W = D_BR // 4
C_GROUPS = 4
C_GROUP_W = D_BR // C_GROUPS
SPATIAL_CHUNK = 128
PAST_LEN = 4096
RMS_EPS = 1e-6
LN_EPS = 1e-5

COL_A = (0, 2 * D_BR)
COL_B = (2 * D_BR, 5 * D_BR)
COL_C = (5 * D_BR, 7 * D_BR)
COL_D = (7 * D_BR, 8 * D_BR)
BR_COLS = 8 * D_BR

PAD_A = 32
PAD_B = 8
PAD_D = 24
SUBLANES = 8
ROW_BLOCK = 32

V7X_VMEM_LIMIT = 56 * 1024 * 1024

BF16 = jnp.bfloat16
F32 = jnp.float32


def _dot(a, b):
    return jnp.dot(a, b, preferred_element_type=F32)


def _rmsnorm(x, g):
    ms = jnp.mean(x * x, axis=-1, keepdims=True)
    return x * lax.rsqrt(ms + RMS_EPS) * g


def _layernorm(x, g, b):
    mu = jnp.mean(x, axis=-1, keepdims=True)
    xc = x - mu
    var = jnp.mean(xc * xc, axis=-1, keepdims=True)
    return xc * lax.rsqrt(var + LN_EPS) * g + b


def _sigmoid(x):
    return 0.5 * (jnp.tanh(0.5 * x) + 1.0)


def _silu(x):
    return x * _sigmoid(x)


def _gelu_erf(x):
    return 0.5 * x * (1.0 + lax.erf(x * np.float32(np.sqrt(0.5))))


def _windowed_sum(ext_ref, seg, first_row, rows, taps, weights=None, cols=slice(None), rb=ROW_BLOCK):
    outs = []
    rb = min(rb, rows)
    for r0 in range(0, rows, rb):
        acc = None
        for k in range(taps):
            term = ext_ref[seg, pl.ds(first_row + r0 + k, rb), cols]
            if weights is not None:
                term = term * weights[k]
            acc = term if acc is None else acc + term
        outs.append(acc)
    return outs[0] if len(outs) == 1 else jnp.concatenate(outs, axis=0)


def _conv_a(ea_ref, sh_ref, seg, weights, rows):
    ha = CONV_A_WIDTH - 1
    span = rows + PAD_A - SUBLANES
    for r in range(1, SUBLANES):
        sh_ref[r - 1, seg] = ea_ref[seg, pl.ds(r, span), :]
    outs = []
    rb = min(ROW_BLOCK, rows)
    for r0 in range(0, rows, rb):
        acc = None
        for k in range(CONV_A_WIDTH):
            first = PAD_A - ha + k
            r = first % SUBLANES
            if r == 0:
                term = ea_ref[seg, pl.ds(first + r0, rb), :]
            else:
                term = sh_ref[r - 1, seg, pl.ds(first - r + r0, rb), :]
            term = term * weights[k]
            acc = term if acc is None else acc + term
        outs.append(acc)
    return outs[0] if len(outs) == 1 else jnp.concatenate(outs, axis=0)


def _mix_kernel(x_ref, g_ref, w_ref, caw_ref, cab_ref, lnag_ref, lnab_ref, cbw_ref, lncg_ref, lncb_ref,
                wmix_ref, bmix_ref, poolw_ref, pscale_ref, ha_ref, hb_ref, hd_ref,
                h_ref, acts_ref, na_ref, nb_ref, nd_ref, vr_ref,
                ea_ref, eb_ref, ed_ref, sh_ref, pw_ref, *, nseg, seg_len, start, mix_chunk, vrows):
    i = pl.program_id(1)
    L = seg_len
    rows = nseg * L
    ha, hb, hd = CONV_A_WIDTH - 1, CONV_B_WIDTH - 1, POOL_HIST

    @pl.when(i == 0)
    def _load_history():
        ea_ref[:, 0:PAD_A - ha, :] = jnp.zeros((nseg, PAD_A - ha, D_BR), F32)
        ea_ref[:, PAD_A - ha:PAD_A, :] = ha_ref[...]
        eb_ref[:, PAD_B - hb:PAD_B, :] = hb_ref[...]
        ed_ref[:, 0:PAD_D - hd, :] = jnp.zeros((nseg, PAD_D - hd, D_BR), F32)
        ed_ref[:, PAD_D - hd:PAD_D, :] = hd_ref[...]
        pw_ref[:, :, 0:SUBLANES, :] = jnp.zeros((2, nseg, SUBLANES, D_BR), F32)

    caw = [caw_ref[k:k + 1, :] for k in range(CONV_A_WIDTH)]
    cbw = [cbw_ref[k:k + 1, :] for k in range(CONV_B_WIDTH)]
    segs = [(s, s * L) for s in range(nseg)]

    h = _rmsnorm(x_ref[...], g_ref[...]).astype(BF16)
    h_ref[...] = h

    za = _dot(h, w_ref[:, COL_A[0]:COL_A[1]])
    a = za[:, :D_BR] * _sigmoid(za[:, D_BR:])
    for s, q0 in segs:
        ea_ref[s, PAD_A:PAD_A + L, :] = a[q0:q0 + L]
        conv = _conv_a(ea_ref, sh_ref, s, caw, L)
        a_act = _silu(_layernorm(conv + cab_ref[...], lnag_ref[...], lnab_ref[...]))
        acts_ref[q0:q0 + L, 0:D_BR] = a_act.astype(BF16)

    zb = _dot(h, w_ref[:, COL_B[0]:COL_B[1]])
    m = zb[:, D_BR:2 * D_BR] * zb[:, 2 * D_BR:]
    for s, q0 in segs:
        eb_ref[s, PAD_B:PAD_B + L, :] = m[q0:q0 + L]
        conv = _windowed_sum(eb_ref, s, PAD_B - hb, L, CONV_B_WIDTH, cbw)
        acts_ref[q0:q0 + L, D_BR:2 * D_BR] = (zb[q0:q0 + L, :D_BR] * conv).astype(BF16)

    zc = _gelu_erf(_dot(h, w_ref[:, COL_C[0]:COL_C[1]]))
    u = zc[:, :D_BR]
    v = _layernorm(zc[:, D_BR:], lncg_ref[...], lncb_ref[...])
    for s, q0 in segs:
        vr_ref[s] = v[q0 + L - vrows:q0 + L]
    vb = v.astype(BF16)
    for c0 in range(0, rows, mix_chunk):
        for g in range(C_GROUPS):
            cols = slice(g * C_GROUP_W, (g + 1) * C_GROUP_W)
            mixed = _dot(wmix_ref[g], vb[c0:c0 + mix_chunk, cols]) + bmix_ref[:, cols]
            acts_ref[c0:c0 + mix_chunk, 2 * D_BR + g * C_GROUP_W:2 * D_BR + (g + 1) * C_GROUP_W] = (
                u[c0:c0 + mix_chunk, cols] * mixed).astype(BF16)

    zd = _dot(h, w_ref[:, COL_D[0]:COL_D[1]])
    for s, q0 in segs:
        ed_ref[s, PAD_D:PAD_D + L, :] = zd[q0:q0 + L]
        pos = start + i * L + lax.broadcasted_iota(jnp.int32, (L, POOL_GROUP_W), 0)
        span = PAD_D + L - SUBLANES
        g1, g2, g3 = (slice(g * POOL_GROUP_W, D_BR) for g in (1, 2, 3))
        s2 = _windowed_sum(ed_ref, s, SUBLANES - 1, span, 2, rb=PAD_D - SUBLANES)
        pw_ref[0, s, SUBLANES:, g1] = s2[:, g1]
        s4 = pw_ref[0, s, SUBLANES:, g1] + pw_ref[0, s, pl.ds(SUBLANES - 2, span), g1]
        pw_ref[1, s, SUBLANES:, g2] = s4[:, POOL_GROUP_W:]
        s8 = pw_ref[1, s, SUBLANES:, g2] + pw_ref[1, s, pl.ds(SUBLANES - 4, span), g2]
        pw_ref[0, s, SUBLANES:, g3] = s8[:, POOL_GROUP_W:]
        s16 = pw_ref[0, s, PAD_D:, g3] + pw_ref[0, s, PAD_D - SUBLANES:PAD_D - SUBLANES + L, g3]
        t0 = PAD_D - SUBLANES
        sums = (s2[t0:, :POOL_GROUP_W], s4[t0:, :POOL_GROUP_W], s8[t0:, :POOL_GROUP_W], s16)
        for g, wdw in enumerate(POOL_WINDOWS):
            cols = slice(g * POOL_GROUP_W, (g + 1) * POOL_GROUP_W)
            ssum = sums[g]
            cnt = jnp.minimum(pos + 1, wdw).astype(F32)
            pooled = ssum / cnt - zd[q0:q0 + L, cols]
            d_mix = _dot(pooled.astype(BF16), poolw_ref[g])
            acts_ref[q0:q0 + L, 3 * D_BR + g * POOL_GROUP_W:3 * D_BR + (g + 1) * POOL_GROUP_W] = (
                d_mix * pscale_ref[:, cols]).astype(BF16)

    for s in range(nseg):
        for ext_ref, new_ref, pad, hist in ((ea_ref, na_ref, PAD_A, ha), (eb_ref, nb_ref, PAD_B, hb),
                                            (ed_ref, nd_ref, PAD_D, hd)):
            tail = ext_ref[s, pad + L - hist:pad + L, :]
            new_ref[s] = tail
            ext_ref[s, pad - hist:pad, :] = tail


def _mix_call(x, hist_a, hist_b, hist_d, wts, wmix, bmix, *, layer, nseg, seg_len, start):
    nb, t, _ = x.shape
    tm = nseg * seg_len
    n_tiles = t // tm
    assert t % tm == 0 and (nseg == 1 or n_tiles == 1)
    mix_chunk = wmix.shape[-1]
    assert tm % mix_chunk == 0
    vrows = min(seg_len, SPATIAL_CHUNK)

    def of_layer(arr):
        return pl.BlockSpec((None,) + arr.shape[1:], lambda b, i, _n=arr.ndim: (layer,) + (0,) * (_n - 1))

    def per_batch(arr):
        return pl.BlockSpec((None,) + arr.shape[1:], lambda b, i, _n=arr.ndim: (b,) + (0,) * (_n - 1))

    tile = pl.BlockSpec((None, tm, D_MODEL), lambda b, i: (b, i, 0))
    w_br_spec = pl.BlockSpec((None, D_MODEL, BR_COLS), lambda b, i: (layer, 0, 0),
                             pipeline_mode=pl.Buffered(1))
    small = [wts['conv_a_w'], wts['conv_a_b'], wts['ln_a_g'], wts['ln_a_b'], wts['conv_b_w'], wts['ln_c_g'],
             wts['ln_c_b'], wmix, bmix, wts['pool_w'], wts['pool_scale']]
    out_shape = (
        jax.ShapeDtypeStruct((nb, t, D_MODEL), BF16),
        jax.ShapeDtypeStruct((nb, t, D_MODEL), BF16),
        jax.ShapeDtypeStruct((nb, nseg, CONV_A_WIDTH - 1, D_BR), F32),
        jax.ShapeDtypeStruct((nb, nseg, CONV_B_WIDTH - 1, D_BR), F32),
        jax.ShapeDtypeStruct((nb, nseg, POOL_HIST, D_BR), F32),
        jax.ShapeDtypeStruct((nb, nseg, vrows, D_BR), F32),
    )
    kern = functools.partial(_mix_kernel, nseg=nseg, seg_len=seg_len, start=start, mix_chunk=mix_chunk,
                             vrows=vrows)
    return pl.pallas_call(
        kern,
        grid=(nb, n_tiles),
        in_specs=[tile, of_layer(wts['norm_mix_g']), w_br_spec] + [of_layer(a) for a in small]
        + [per_batch(hist_a), per_batch(hist_b), per_batch(hist_d)],
        out_specs=(tile, tile) + tuple(per_batch(s) for s in out_shape[2:]),
        out_shape=out_shape,
        scratch_shapes=[
            pltpu.VMEM((nseg, PAD_A + seg_len, D_BR), F32),
            pltpu.VMEM((nseg, PAD_B + seg_len, D_BR), F32),
            pltpu.VMEM((nseg, PAD_D + seg_len, D_BR), F32),
            pltpu.VMEM((SUBLANES - 1, nseg, PAD_A - SUBLANES + seg_len, D_BR), F32),
            pltpu.VMEM((2, nseg, PAD_D + seg_len, D_BR), F32),
        ],
        compiler_params=pltpu.CompilerParams(dimension_semantics=("arbitrary", "arbitrary"),
                                             vmem_limit_bytes=V7X_VMEM_LIMIT),
        name="mix",
    )(x, wts['norm_mix_g'], wts['w_in'], *small, hist_a, hist_b, hist_d)


def _gate_kernel(h_ref, acts_ref, wg0, wg1, wg2, wg3, wo0, wo1, wo2, wo3, out_ref):
    h = h_ref[...]
    acc = None
    for b, (wg, wo) in enumerate(((wg0, wo0), (wg1, wo1), (wg2, wo2), (wg3, wo3))):
        gate = _sigmoid(_dot(h, wg[...]))
        term = gate * _dot(acts_ref[:, b * D_BR:(b + 1) * D_BR], wo[...])
        acc = term if acc is None else acc + term
    out_ref[...] = acc.astype(BF16)


def _gate_call(h, acts, wts, *, layer, tm, nc):
    n = h.shape[0]
    n_col = D_MODEL // nc
    gate0 = BR_COLS // nc
    act_tile = pl.BlockSpec((tm, D_MODEL), lambda i, j: (i, 0))
    wg_specs = [pl.BlockSpec((None, D_MODEL, nc), lambda i, j, _b=b: (layer, 0, gate0 + _b * n_col + j))
                for b in range(4)]
    wo_spec = pl.BlockSpec((None, D_BR, nc), lambda i, j: (layer, 0, j))
    return pl.pallas_call(
        _gate_kernel,
        grid=(n // tm, n_col),
        in_specs=[act_tile, act_tile] + wg_specs + [wo_spec] * 4,
        out_specs=pl.BlockSpec((tm, nc), lambda i, j: (i, j)),
        out_shape=jax.ShapeDtypeStruct((n, D_MODEL), BF16),
        compiler_params=pltpu.CompilerParams(dimension_semantics=("parallel", "arbitrary"),
                                             vmem_limit_bytes=V7X_VMEM_LIMIT),
        name="gate",
    )(h, acts, wts['w_in'], wts['w_in'], wts['w_in'], wts['w_in'],
      wts['w_out_a'], wts['w_out_b'], wts['w_out_c'], wts['w_out_d'])


def _oproj_kernel(x_ref, m_ref, wo_ref, out_ref):
    out_ref[...] = x_ref[...] + _dot(m_ref[...], wo_ref[...])


def _oproj_call(x, merged, wts, *, layer, tm):
    n = x.shape[0]
    return pl.pallas_call(
        _oproj_kernel,
        grid=(n // tm,),
        in_specs=[pl.BlockSpec((tm, D_MODEL), lambda i: (i, 0)),
                  pl.BlockSpec((tm, D_MODEL), lambda i: (i, 0)),
                  pl.BlockSpec((None, D_MODEL, D_MODEL), lambda i: (layer, 0, 0),
                               pipeline_mode=pl.Buffered(1))],
        out_specs=pl.BlockSpec((tm, D_MODEL), lambda i: (i, 0)),
        out_shape=jax.ShapeDtypeStruct((n, D_MODEL), F32),
        compiler_params=pltpu.CompilerParams(dimension_semantics=("parallel",),
                                             vmem_limit_bytes=V7X_VMEM_LIMIT),
        name="oproj",
    )(x, merged, wts['w_o'])


def _ffn_kernel(x_ref, g_ref, w1_ref, w3_ref, w2_ref, gfin_ref, out_ref, h2_ref, *, final_norm):
    j = pl.program_id(1)

    @pl.when(j == 0)
    def _start():
        x = x_ref[...]
        h2_ref[...] = _rmsnorm(x, g_ref[...]).astype(BF16)
        out_ref[...] = x

    h2 = h2_ref[...]
    hidden = (_silu(_dot(h2, w1_ref[...])) * _dot(h2, w3_ref[...])).astype(BF16)
    out_ref[...] += _dot(hidden, w2_ref[...])

    if final_norm:
        @pl.when(j == pl.num_programs(1) - 1)
        def _finish():
            out_ref[...] = _rmsnorm(out_ref[...], gfin_ref[...])


def _ffn_call(x, wts, norm_final_g, *, layer, tm, fc, final_norm):
    n = x.shape[0]
    kern = functools.partial(_ffn_kernel, final_norm=final_norm)
    return pl.pallas_call(
        kern,
        grid=(n // tm, D_FF // fc),
        in_specs=[pl.BlockSpec((tm, D_MODEL), lambda i, j: (i, 0)),
                  pl.BlockSpec((None, 1, D_MODEL), lambda i, j: (layer, 0, 0)),
                  pl.BlockSpec((None, D_MODEL, fc), lambda i, j: (layer, 0, j)),
                  pl.BlockSpec((None, D_MODEL, fc), lambda i, j: (layer, 0, j)),
                  pl.BlockSpec((None, fc, D_MODEL), lambda i, j: (layer, j, 0)),
                  pl.BlockSpec((1, D_MODEL), lambda i, j: (0, 0))],
        out_specs=pl.BlockSpec((tm, D_MODEL), lambda i, j: (i, 0)),
        out_shape=jax.ShapeDtypeStruct((n, D_MODEL), F32),
        scratch_shapes=[pltpu.VMEM((tm, D_MODEL), BF16)],
        compiler_params=pltpu.CompilerParams(dimension_semantics=("parallel", "arbitrary"),
                                             vmem_limit_bytes=V7X_VMEM_LIMIT),
        name="ffn",
    )(x, wts['norm_ffn_g'], wts['ffn_w1'], wts['ffn_w3'], wts['ffn_w2'], norm_final_g)


def _spatial_mix_operands(spatial_w, spatial_b, seg_len):
    depth = spatial_w.shape[0]
    tri = jnp.tril(jnp.ones((SPATIAL_CHUNK, SPATIAL_CHUNK), dtype=bool))
    wm = jnp.where(tri, spatial_w, jnp.zeros((), spatial_w.dtype))
    bias = jnp.repeat(jnp.swapaxes(spatial_b, 1, 2), C_GROUP_W, axis=2)
    if seg_len >= SPATIAL_CHUNK:
        assert seg_len % SPATIAL_CHUNK == 0
        return wm.astype(BF16), bias
    reps = 2 * SPATIAL_CHUNK // seg_len
    eye = jnp.eye(reps, dtype=wm.dtype)
    corner = wm[:, :, :seg_len, :seg_len]
    blockdiag = jnp.einsum('pq,lgts->lgptqs', eye, corner).reshape(
        depth, C_GROUPS, reps * seg_len, reps * seg_len)
    return blockdiag.astype(BF16), jnp.tile(bias[:, :seg_len], (1, reps, 1))


PROMPT_MIX_TILE = 512
PROMPT_TILE = 1024
GATE_COLS = 512
FFN_COLS = 512
SAMPLE_GATE_COLS = 1024
SAMPLE_FFN_COLS = 1408


def kernel(x_prompt, x_sample, state_conv_a, state_conv_b, state_pool, norm_mix_g, w_in, conv_a_w, conv_a_b,
           ln_a_g, ln_a_b, w_out_a, conv_b_w, w_out_b, ln_c_g, ln_c_b, spatial_w, spatial_b, w_out_c, pool_w,
           pool_scale, w_out_d, w_o, norm_ffn_g, ffn_w1, ffn_w3, ffn_w2, norm_final_g):
    depth = w_in.shape[0]
    nbp, seq, _ = x_prompt.shape
    nbs, dseq, _ = x_sample.shape

    rows = lambda v: v.reshape(depth, 1, -1)
    wts = dict(
        w_in=w_in.astype(BF16),
        w_out_a=w_out_a.astype(BF16), w_out_b=w_out_b.astype(BF16),
        w_out_c=w_out_c.astype(BF16), w_out_d=w_out_d.astype(BF16),
        w_o=w_o.astype(BF16), pool_w=pool_w.astype(BF16),
        ffn_w1=ffn_w1.astype(BF16), ffn_w3=ffn_w3.astype(BF16), ffn_w2=ffn_w2.astype(BF16),
        norm_mix_g=rows(norm_mix_g), norm_ffn_g=rows(norm_ffn_g),
        conv_a_w=conv_a_w, conv_a_b=rows(conv_a_b), ln_a_g=rows(ln_a_g), ln_a_b=rows(ln_a_b),
        conv_b_w=conv_b_w, ln_c_g=rows(ln_c_g), ln_c_b=rows(ln_c_b), pool_scale=rows(pool_scale),
    )
    gfin = norm_final_g.reshape(1, -1)

    def run_group(x, hist_a, hist_b, hist_d, *, nseg, seg_len, start, tm, gate_cols, ffn_cols):
        nb, t, _ = x.shape
        wmix, bmix = _spatial_mix_operands(spatial_w, spatial_b, seg_len)
        states = []
        for l in range(depth):
            h, acts, na, nbuf, nd, vr = _mix_call(x, hist_a[l], hist_b[l], hist_d[l], wts, wmix, bmix,
                                                  layer=l, nseg=nseg, seg_len=seg_len, start=start)
            xf = x.reshape(nb * t, D_MODEL)
            merged = _gate_call(h.reshape(nb * t, D_MODEL), acts.reshape(nb * t, D_MODEL), wts,
                                layer=l, tm=tm, nc=gate_cols)
            x1 = _oproj_call(xf, merged, wts, layer=l, tm=tm)
            x = _ffn_call(x1, wts, gfin, layer=l, tm=tm, fc=ffn_cols,
                          final_norm=(l == depth - 1)).reshape(nb, t, D_MODEL)
            states.append((na, nbuf, nd, vr))
        return (x,) + tuple(jnp.stack(s) for s in zip(*states))

    dt = x_prompt.dtype
    zeros = lambda rows: jnp.zeros((depth, nbp, 1, rows, D_BR), dt)
    yp, ap, bp, dp, vp = run_group(x_prompt, zeros(CONV_A_WIDTH - 1), zeros(CONV_B_WIDTH - 1), zeros(POOL_HIST),
                                   nseg=1, seg_len=PROMPT_MIX_TILE, start=0, tm=PROMPT_TILE, gate_cols=GATE_COLS,
                                   ffn_cols=FFN_COLS)
    ys, a_s, b_s, d_s, v_s = run_group(
        x_sample.reshape(1, nbs * dseq, D_MODEL), state_conv_a[:, None], state_conv_b[:, None],
        state_pool[:, None], nseg=nbs, seg_len=dseq, start=PAST_LEN, tm=nbs * dseq, gate_cols=SAMPLE_GATE_COLS,
        ffn_cols=SAMPLE_FFN_COLS)

    squeeze_p = lambda s: s.reshape(depth, nbp, s.shape[-2], D_BR)
    squeeze_s = lambda s: s.reshape(depth, nbs, s.shape[-2], D_BR)
    return (yp, ys.reshape(nbs, dseq, D_MODEL), squeeze_p(ap), squeeze_s(a_s), squeeze_p(bp), squeeze_s(b_s),
            squeeze_p(dp), squeeze_s(d_s), squeeze_p(vp), squeeze_s(v_s))
```

```python
import functools

import jax
import jax.numpy as jnp
import numpy as np
from jax import lax
from jax.experimental import pallas as pl
from jax.experimental.pallas import tpu as pltpu

D_MODEL = 2048
D_BR = D_MODEL // 4
D_FF = 5632
CONV_A_WIDTH = 31
CONV_B_WIDTH = 3
POOL_WINDOWS = (2, 4, 8, 16)
POOL_HIST = 15
POOL_GROUP_W = D_BR // 4
C_GROUPS = 4
C_GROUP_W = D_BR // C_GROUPS
SPATIAL_CHUNK = 128
PAST_LEN = 4096
RMS_EPS = 1e-6
LN_EPS = 1e-5

COL_A = (0, 2 * D_BR)
COL_B = (2 * D_BR, 5 * D_BR)
COL_C = (5 * D_BR, 7 * D_BR)
COL_D = (7 * D_BR, 8 * D_BR)
BR_COLS = 8 * D_BR

PAD_A = 32
PAD_B = 8
PAD_D = 24
SUBLANES = 8
ROW_BLOCK = 32

P_CONV_A_W = 0
P_CONV_A_B = P_CONV_A_W + CONV_A_WIDTH
P_LN_A_G = P_CONV_A_B + 1
P_LN_A_B = P_LN_A_G + 1
P_CONV_B_W = P_LN_A_B + 1
P_LN_C_G = P_CONV_B_W + CONV_B_WIDTH
P_LN_C_B = P_LN_C_G + 1
P_POOL_SCALE = P_LN_C_B + 1
P_MIX_BIAS = P_POOL_SCALE + 1
H_CONV_A = 0
H_CONV_B = H_CONV_A + CONV_A_WIDTH - 1
H_POOL = H_CONV_B + CONV_B_WIDTH - 1
H_ROWS = H_POOL + POOL_HIST

V7X_VMEM_LIMIT = 56 * 1024 * 1024

BF16 = jnp.bfloat16
F32 = jnp.float32


def _dot(a, b):
    return jnp.dot(a, b, preferred_element_type=F32)


def _rmsnorm(x, g):
    ms = jnp.mean(x * x, axis=-1, keepdims=True)
    return x * lax.rsqrt(ms + RMS_EPS) * g


def _layernorm(x, g, b):
    mu = jnp.mean(x, axis=-1, keepdims=True)
    xc = x - mu
    var = jnp.mean(xc * xc, axis=-1, keepdims=True)
    return xc * lax.rsqrt(var + LN_EPS) * g + b


def _sigmoid(x):
    return 0.5 * (jnp.tanh(0.5 * x) + 1.0)


def _silu(x):
    return x * _sigmoid(x)


def _gelu_erf(x):
    return 0.5 * x * (1.0 + lax.erf(x * np.float32(np.sqrt(0.5))))


def _windowed_sum(ext_ref, seg, first_row, rows, taps, weights=None, cols=slice(None), rb=ROW_BLOCK):
    outs = []
    rb = min(rb, rows)
    for r0 in range(0, rows, rb):
        acc = None
        for k in range(taps):
            term = ext_ref[seg, pl.ds(first_row + r0 + k, rb), cols]
            if weights is not None:
                term = term * weights[k]
            acc = term if acc is None else acc + term
        outs.append(acc)
    return outs[0] if len(outs) == 1 else jnp.concatenate(outs, axis=0)


def _conv_a(ea_ref, sh_ref, seg, weights, rows):
    ha = CONV_A_WIDTH - 1
    span = rows + PAD_A - SUBLANES
    for r in range(1, SUBLANES):
        sh_ref[r - 1, seg] = ea_ref[seg, pl.ds(r, span), :]
    outs = []
    rb = min(ROW_BLOCK, rows)
    for r0 in range(0, rows, rb):
        acc = None
        for k in range(CONV_A_WIDTH):
            first = PAD_A - ha + k
            r = first % SUBLANES
            if r == 0:
                term = ea_ref[seg, pl.ds(first + r0, rb), :]
            else:
                term = sh_ref[r - 1, seg, pl.ds(first - r + r0, rb), :]
            term = term * weights[k]
            acc = term if acc is None else acc + term
        outs.append(acc)
    return outs[0] if len(outs) == 1 else jnp.concatenate(outs, axis=0)


def _mix_kernel(x_ref, g_ref, w_ref, p_ref, wmix_ref, poolw_ref, hist_ref,
                h_ref, acts_ref, na_ref, nb_ref, nd_ref, vr_ref,
                ea_ref, eb_ref, ed_ref, sh_ref, pw_ref, *, nseg, seg_len, start, mix_chunk, vrows):
    i = pl.program_id(1)
    L = seg_len
    rows = nseg * L
    ha, hb, hd = CONV_A_WIDTH - 1, CONV_B_WIDTH - 1, POOL_HIST

    @pl.when(i == 0)
    def _load_history():
        ea_ref[:, 0:PAD_A - ha, :] = jnp.zeros((nseg, PAD_A - ha, D_BR), F32)
        ea_ref[:, PAD_A - ha:PAD_A, :] = hist_ref[:, H_CONV_A:H_CONV_A + ha, :]
        eb_ref[:, PAD_B - hb:PAD_B, :] = hist_ref[:, H_CONV_B:H_CONV_B + hb, :]
        ed_ref[:, 0:PAD_D - hd, :] = jnp.zeros((nseg, PAD_D - hd, D_BR), F32)
        ed_ref[:, PAD_D - hd:PAD_D, :] = hist_ref[:, H_POOL:H_POOL + hd, :]
        pw_ref[:, :, 0:SUBLANES, :] = jnp.zeros((2, nseg, SUBLANES, D_BR), F32)

    row = lambda r: p_ref[r:r + 1, :]
    caw = [row(P_CONV_A_W + k) for k in range(CONV_A_WIDTH)]
    cbw = [row(P_CONV_B_W + k) for k in range(CONV_B_WIDTH)]
    segs = [(s, s * L) for s in range(nseg)]

    h = _rmsnorm(x_ref[...], g_ref[...]).astype(BF16)
    h_ref[...] = h

    za = _dot(h, w_ref[:, COL_A[0]:COL_A[1]])
    a = za[:, :D_BR] * _sigmoid(za[:, D_BR:])
    for s, q0 in segs:
        ea_ref[s, PAD_A:PAD_A + L, :] = a[q0:q0 + L]
        conv = _conv_a(ea_ref, sh_ref, s, caw, L)
        a_act = _silu(_layernorm(conv + row(P_CONV_A_B), row(P_LN_A_G), row(P_LN_A_B)))
        acts_ref[q0:q0 + L, 0:D_BR] = a_act.astype(BF16)

    zb = _dot(h, w_ref[:, COL_B[0]:COL_B[1]])
    m = zb[:, D_BR:2 * D_BR] * zb[:, 2 * D_BR:]
    for s, q0 in segs:
        eb_ref[s, PAD_B:PAD_B + L, :] = m[q0:q0 + L]
        conv = _windowed_sum(eb_ref, s, PAD_B - hb, L, CONV_B_WIDTH, cbw)
        acts_ref[q0:q0 + L, D_BR:2 * D_BR] = (zb[q0:q0 + L, :D_BR] * conv).astype(BF16)

    zc = _gelu_erf(_dot(h, w_ref[:, COL_C[0]:COL_C[1]]))
    u = zc[:, :D_BR]
    v = _layernorm(zc[:, D_BR:], row(P_LN_C_G), row(P_LN_C_B))
    for s, q0 in segs:
        vr_ref[s] = v[q0 + L - vrows:q0 + L]
    vb = v.astype(BF16)
    for c0 in range(0, rows, mix_chunk):
        for g in range(C_GROUPS):
            cols = slice(g * C_GROUP_W, (g + 1) * C_GROUP_W)
            mixed = _dot(wmix_ref[g], vb[c0:c0 + mix_chunk, cols]) + p_ref[P_MIX_BIAS:P_MIX_BIAS + mix_chunk, cols]
            acts_ref[c0:c0 + mix_chunk, 2 * D_BR + g * C_GROUP_W:2 * D_BR + (g + 1) * C_GROUP_W] = (
                u[c0:c0 + mix_chunk, cols] * mixed).astype(BF16)

    zd = _dot(h, w_ref[:, COL_D[0]:COL_D[1]])
    for s, q0 in segs:
        ed_ref[s, PAD_D:PAD_D + L, :] = zd[q0:q0 + L]
        pos = start + i * L + lax.broadcasted_iota(jnp.int32, (L, POOL_GROUP_W), 0)
        span = PAD_D + L - SUBLANES
        g1, g2, g3 = (slice(g * POOL_GROUP_W, D_BR) for g in (1, 2, 3))
        s2 = _windowed_sum(ed_ref, s, SUBLANES - 1, span, 2, rb=PAD_D - SUBLANES)
        pw_ref[0, s, SUBLANES:, g1] = s2[:, g1]
        s4 = pw_ref[0, s, SUBLANES:, g1] + pw_ref[0, s, pl.ds(SUBLANES - 2, span), g1]
        pw_ref[1, s, SUBLANES:, g2] = s4[:, POOL_GROUP_W:]
        s8 = pw_ref[1, s, SUBLANES:, g2] + pw_ref[1, s, pl.ds(SUBLANES - 4, span), g2]
        pw_ref[0, s, SUBLANES:, g3] = s8[:, POOL_GROUP_W:]
        s16 = pw_ref[0, s, PAD_D:, g3] + pw_ref[0, s, PAD_D - SUBLANES:PAD_D - SUBLANES + L, g3]
        t0 = PAD_D - SUBLANES
        sums = (s2[t0:, :POOL_GROUP_W], s4[t0:, :POOL_GROUP_W], s8[t0:, :POOL_GROUP_W], s16)
        for g, wdw in enumerate(POOL_WINDOWS):
            cols = slice(g * POOL_GROUP_W, (g + 1) * POOL_GROUP_W)
            ssum = sums[g]
            cnt = jnp.minimum(pos + 1, wdw).astype(F32)
            pooled = ssum / cnt - zd[q0:q0 + L, cols]
            d_mix = _dot(pooled.astype(BF16), poolw_ref[g])
            acts_ref[q0:q0 + L, 3 * D_BR + g * POOL_GROUP_W:3 * D_BR + (g + 1) * POOL_GROUP_W] = (
                d_mix * p_ref[P_POOL_SCALE:P_POOL_SCALE + 1, cols]).astype(BF16)

    for s in range(nseg):
        for ext_ref, new_ref, pad, hist in ((ea_ref, na_ref, PAD_A, ha), (eb_ref, nb_ref, PAD_B, hb),
                                            (ed_ref, nd_ref, PAD_D, hd)):
            tail = ext_ref[s, pad + L - hist:pad + L, :]
            new_ref[s] = tail
            ext_ref[s, pad - hist:pad, :] = tail


def _mix_call(x, hist, wts, params, wmix, *, layer, nseg, seg_len, start):
    nb, t, _ = x.shape
    tm = nseg * seg_len
    n_tiles = t // tm
    assert t % tm == 0 and (nseg == 1 or n_tiles == 1)
    mix_chunk = wmix.shape[-1]
    assert tm % mix_chunk == 0
    vrows = min(seg_len, SPATIAL_CHUNK)

    def of_layer(arr):
        return pl.BlockSpec((None,) + arr.shape[1:], lambda b, i, _n=arr.ndim: (layer,) + (0,) * (_n - 1))

    def per_batch(arr):
        return pl.BlockSpec((None,) + arr.shape[1:], lambda b, i, _n=arr.ndim: (b,) + (0,) * (_n - 1))

    tile = pl.BlockSpec((None, tm, D_MODEL), lambda b, i: (b, i, 0))
    w_br_spec = pl.BlockSpec((None, D_MODEL, BR_COLS), lambda b, i: (layer, 0, 0),
                             pipeline_mode=pl.Buffered(1))
    small = [params, wmix, wts['pool_w']]
    out_shape = (
        jax.ShapeDtypeStruct((nb, t, D_MODEL), BF16),
        jax.ShapeDtypeStruct((nb, t, D_MODEL), BF16),
        jax.ShapeDtypeStruct((nb, nseg, CONV_A_WIDTH - 1, D_BR), F32),
        jax.ShapeDtypeStruct((nb, nseg, CONV_B_WIDTH - 1, D_BR), F32),
        jax.ShapeDtypeStruct((nb, nseg, POOL_HIST, D_BR), F32),
        jax.ShapeDtypeStruct((nb, nseg, vrows, D_BR), F32),
    )
    kern = functools.partial(_mix_kernel, nseg=nseg, seg_len=seg_len, start=start, mix_chunk=mix_chunk,
                             vrows=vrows)
    return pl.pallas_call(
        kern,
        grid=(nb, n_tiles),
        in_specs=[tile, of_layer(wts['norm_mix_g']), w_br_spec] + [of_layer(a) for a in small]
        + [per_batch(hist)],
        out_specs=(tile, tile) + tuple(per_batch(s) for s in out_shape[2:]),
        out_shape=out_shape,
        scratch_shapes=[
            pltpu.VMEM((nseg, PAD_A + seg_len, D_BR), F32),
            pltpu.VMEM((nseg, PAD_B + seg_len, D_BR), F32),
            pltpu.VMEM((nseg, PAD_D + seg_len, D_BR), F32),
            pltpu.VMEM((SUBLANES - 1, nseg, PAD_A - SUBLANES + seg_len, D_BR), F32),
            pltpu.VMEM((2, nseg, PAD_D + seg_len, D_BR), F32),
        ],
        compiler_params=pltpu.CompilerParams(dimension_semantics=("arbitrary", "arbitrary"),
                                             vmem_limit_bytes=V7X_VMEM_LIMIT),
        name="mix",
    )(x, wts['norm_mix_g'], wts['w_in'], *small, hist)


def _gate_kernel(h_ref, acts_ref, wg0, wg1, wg2, wg3, wo_ref, out_ref):
    h = h_ref[...]
    acc = None
    for b, wg in enumerate((wg0, wg1, wg2, wg3)):
        gate = _sigmoid(_dot(h, wg[...]))
        term = gate * _dot(acts_ref[:, b * D_BR:(b + 1) * D_BR], wo_ref[b])
        acc = term if acc is None else acc + term
    out_ref[...] = acc.astype(BF16)


def _gate_call(h, acts, wts, *, layer, tm, nc):
    n = h.shape[0]
    n_col = D_MODEL // nc
    gate0 = BR_COLS // nc
    act_tile = pl.BlockSpec((tm, D_MODEL), lambda i, j: (i, 0))
    wg_specs = [pl.BlockSpec((None, D_MODEL, nc), lambda i, j, _b=b: (layer, 0, gate0 + _b * n_col + j))
                for b in range(4)]
    wo_spec = pl.BlockSpec((None, 4, D_BR, nc), lambda i, j: (layer, 0, 0, j))
    return pl.pallas_call(
        _gate_kernel,
        grid=(n // tm, n_col),
        in_specs=[act_tile, act_tile] + wg_specs + [wo_spec],
        out_specs=pl.BlockSpec((tm, nc), lambda i, j: (i, j)),
        out_shape=jax.ShapeDtypeStruct((n, D_MODEL), BF16),
        compiler_params=pltpu.CompilerParams(dimension_semantics=("parallel", "arbitrary"),
                                             vmem_limit_bytes=V7X_VMEM_LIMIT),
        name="gate",
    )(h, acts, wts['w_in'], wts['w_in'], wts['w_in'], wts['w_in'], wts['w_out'])


def _oproj_kernel(x_ref, m_ref, wo_ref, out_ref):
    out_ref[...] = x_ref[...] + _dot(m_ref[...], wo_ref[...])


def _oproj_call(x, merged, wts, *, layer, tm):
    n = x.shape[0]
    return pl.pallas_call(
        _oproj_kernel,
        grid=(n // tm,),
        in_specs=[pl.BlockSpec((tm, D_MODEL), lambda i: (i, 0)),
                  pl.BlockSpec((tm, D_MODEL), lambda i: (i, 0)),
                  pl.BlockSpec((None, D_MODEL, D_MODEL), lambda i: (layer, 0, 0),
                               pipeline_mode=pl.Buffered(1))],
        out_specs=pl.BlockSpec((tm, D_MODEL), lambda i: (i, 0)),
        out_shape=jax.ShapeDtypeStruct((n, D_MODEL), F32),
        compiler_params=pltpu.CompilerParams(dimension_semantics=("parallel",),
                                             vmem_limit_bytes=V7X_VMEM_LIMIT),
        name="oproj",
    )(x, merged, wts['w_o'])


def _ffn_kernel(x_ref, g_ref, w1_ref, w3_ref, w2_ref, gfin_ref, out_ref, h2_ref, *, final_norm):
    j = pl.program_id(1)

    @pl.when(j == 0)
    def _start():
        x = x_ref[...]
        h2_ref[...] = _rmsnorm(x, g_ref[...]).astype(BF16)
        out_ref[...] = x

    h2 = h2_ref[...]
    hidden = (_silu(_dot(h2, w1_ref[...])) * _dot(h2, w3_ref[...])).astype(BF16)
    out_ref[...] += _dot(hidden, w2_ref[...])

    if final_norm:
        @pl.when(j == pl.num_programs(1) - 1)
        def _finish():
            out_ref[...] = _rmsnorm(out_ref[...], gfin_ref[...])


def _ffn_call(x, wts, norm_final_g, *, layer, tm, fc, final_norm):
    n = x.shape[0]
    kern = functools.partial(_ffn_kernel, final_norm=final_norm)
    return pl.pallas_call(
        kern,
        grid=(n // tm, D_FF // fc),
        in_specs=[pl.BlockSpec((tm, D_MODEL), lambda i, j: (i, 0)),
                  pl.BlockSpec((None, 1, D_MODEL), lambda i, j: (layer, 0, 0)),
                  pl.BlockSpec((None, D_MODEL, fc), lambda i, j: (layer, 0, j)),
                  pl.BlockSpec((None, D_MODEL, fc), lambda i, j: (layer, 0, j)),
                  pl.BlockSpec((None, fc, D_MODEL), lambda i, j: (layer, j, 0)),
                  pl.BlockSpec((1, D_MODEL), lambda i, j: (0, 0))],
        out_specs=pl.BlockSpec((tm, D_MODEL), lambda i, j: (i, 0)),
        out_shape=jax.ShapeDtypeStruct((n, D_MODEL), F32),
        scratch_shapes=[pltpu.VMEM((tm, D_MODEL), BF16)],
        compiler_params=pltpu.CompilerParams(dimension_semantics=("parallel", "arbitrary"),
                                             vmem_limit_bytes=V7X_VMEM_LIMIT),
        name="ffn",
    )(x, wts['norm_ffn_g'], wts['ffn_w1'], wts['ffn_w3'], wts['ffn_w2'], norm_final_g)


def _spatial_mix_operands(spatial_w, spatial_b, seg_len):
    depth = spatial_w.shape[0]
    tri = jnp.tril(jnp.ones((SPATIAL_CHUNK, SPATIAL_CHUNK), dtype=bool))
    wm = jnp.where(tri, spatial_w, jnp.zeros((), spatial_w.dtype))
    bias = jnp.repeat(jnp.swapaxes(spatial_b, 1, 2), C_GROUP_W, axis=2)
    if seg_len >= SPATIAL_CHUNK:
        assert seg_len % SPATIAL_CHUNK == 0
        return wm.astype(BF16), bias
    reps = 2 * SPATIAL_CHUNK // seg_len
    eye = jnp.eye(reps, dtype=wm.dtype)
    corner = wm[:, :, :seg_len, :seg_len]
    blockdiag = jnp.einsum('pq,lgts->lgptqs', eye, corner).reshape(
        depth, C_GROUPS, reps * seg_len, reps * seg_len)
    return blockdiag.astype(BF16), jnp.tile(bias[:, :seg_len], (1, reps, 1))


PROMPT_MIX_TILE = 512
PROMPT_TILE = 1024
GATE_COLS = 512
FFN_COLS = 512


def kernel(x_prompt, x_sample, state_conv_a, state_conv_b, state_pool, norm_mix_g, w_in, conv_a_w, conv_a_b,
           ln_a_g, ln_a_b, w_out_a, conv_b_w, w_out_b, ln_c_g, ln_c_b, spatial_w, spatial_b, w_out_c, pool_w,
           pool_scale, w_out_d, w_o, norm_ffn_g, ffn_w1, ffn_w3, ffn_w2, norm_final_g):
    depth = w_in.shape[0]
    nbp, seq, _ = x_prompt.shape
    nbs, dseq, _ = x_sample.shape

    rows = lambda v: v.reshape(depth, 1, -1)
    wts = dict(
        w_in=w_in.astype(BF16),
        w_out=jnp.stack([w_out_a, w_out_b, w_out_c, w_out_d], axis=1).astype(BF16),
        w_o=w_o.astype(BF16), pool_w=pool_w.astype(BF16),
        ffn_w1=ffn_w1.astype(BF16), ffn_w3=ffn_w3.astype(BF16), ffn_w2=ffn_w2.astype(BF16),
        norm_mix_g=rows(norm_mix_g), norm_ffn_g=rows(norm_ffn_g),
    )
    mixer_rows = jnp.concatenate([conv_a_w, rows(conv_a_b), rows(ln_a_g), rows(ln_a_b), conv_b_w, rows(ln_c_g),
                                  rows(ln_c_b), rows(pool_scale)], axis=1)
    gfin = norm_final_g.reshape(1, -1)

    def run_group(x, hist_a, hist_b, hist_d, *, nseg, seg_len, start, tm):
        nb, t, _ = x.shape
        wmix, bmix = _spatial_mix_operands(spatial_w, spatial_b, seg_len)
        params = jnp.concatenate([mixer_rows, bmix], axis=1)
        hist = jnp.concatenate([hist_a, hist_b, hist_d], axis=3)
        states = []
        for l in range(depth):
            h, acts, na, nbuf, nd, vr = _mix_call(x, hist[l], wts, params, wmix, layer=l, nseg=nseg,
                                                  seg_len=seg_len, start=start)
            xf = x.reshape(nb * t, D_MODEL)
            merged = _gate_call(h.reshape(nb * t, D_MODEL), acts.reshape(nb * t, D_MODEL), wts,
                                layer=l, tm=tm, nc=GATE_COLS)
            x1 = _oproj_call(xf, merged, wts, layer=l, tm=tm)
            x = _ffn_call(x1, wts, gfin, layer=l, tm=tm, fc=FFN_COLS,
                          final_norm=(l == depth - 1)).reshape(nb, t, D_MODEL)
            states.append((na, nbuf, nd, vr))
        return (x,) + tuple(jnp.stack(s) for s in zip(*states))

    dt = x_prompt.dtype
    zeros = lambda rows: jnp.zeros((depth, nbp, 1, rows, D_BR), dt)
    yp, ap, bp, dp, vp = run_group(x_prompt, zeros(CONV_A_WIDTH - 1), zeros(CONV_B_WIDTH - 1), zeros(POOL_HIST),
                                   nseg=1, seg_len=PROMPT_MIX_TILE, start=0, tm=PROMPT_TILE)
    ys, a_s, b_s, d_s, v_s = run_group(
        x_sample.reshape(1, nbs * dseq, D_MODEL), state_conv_a[:, None], state_conv_b[:, None],
        state_pool[:, None], nseg=nbs, seg_len=dseq, start=PAST_LEN, tm=nbs * dseq)

    squeeze_p = lambda s: s.reshape(depth, nbp, s.shape[-2], D_BR)
    squeeze_s = lambda s: s.reshape(depth, nbs, s.shape[-2], D_BR)
    return (yp, ys.reshape(nbs, dseq, D_MODEL), squeeze_p(ap), squeeze_s(a_s), squeeze_p(bp), squeeze_s(b_s),
            squeeze_p(dp), squeeze_s(d_s), squeeze_p(vp), squeeze_s(v_s))
```

```python
import functools

import jax
import jax.numpy as jnp
import numpy as np
from jax import lax
from jax.experimental import pallas as pl
from jax.experimental.pallas import tpu as pltpu

D_MODEL = 2048
D_BR = D_MODEL // 4
D_FF = 5632
CONV_A_WIDTH = 31
CONV_B_WIDTH = 3
POOL_WINDOWS = (2, 4, 8, 16)
POOL_HIST = 15
POOL_GROUP_W = D_BR // 4
C_GROUPS = 4
C_GROUP_W = D_BR // C_GROUPS
SPATIAL_CHUNK = 128
PAST_LEN = 4096
RMS_EPS = 1e-6
LN_EPS = 1e-5

COL_A = (0, 2 * D_BR)
COL_B = (2 * D_BR, 5 * D_BR)
COL_C = (5 * D_BR, 7 * D_BR)
COL_D = (7 * D_BR, 8 * D_BR)
BR_COLS = 8 * D_BR

PAD_A = 32
PAD_B = 8
PAD_D = 24
SUBLANES = 8
ROW_BLOCK = 32
CHUNK_ROWS = 64

V7X_VMEM_LIMIT = 56 * 1024 * 1024

BF16 = jnp.bfloat16
F32 = jnp.float32


def _dot(a, b):
    return jnp.dot(a, b, preferred_element_type=F32)


def _rmsnorm(x, g):
    ms = jnp.mean(x * x, axis=-1, keepdims=True)
    return x * lax.rsqrt(ms + RMS_EPS) * g


def _layernorm(x, g, b):
    mu = jnp.mean(x, axis=-1, keepdims=True)
    xc = x - mu
    var = jnp.mean(xc * xc, axis=-1, keepdims=True)
    return xc * lax.rsqrt(var + LN_EPS) * g + b


def _sigmoid(x):
    return 0.5 * (jnp.tanh(0.5 * x) + 1.0)


def _silu(x):
    return x * _sigmoid(x)


def _gelu_erf(x):
    return 0.5 * x * (1.0 + lax.erf(x * np.float32(np.sqrt(0.5))))


def _windowed_sum(ext_ref, seg, first_row, rows, taps, weights=None, cols=slice(None), rb=ROW_BLOCK,
                  consume=None):
    outs = []
    rb = min(rb, rows)
    for r0 in range(0, rows, rb):
        acc = None
        for k in range(taps):
            term = ext_ref[seg, pl.ds(first_row + r0 + k, rb), cols]
            if weights is not None:
                term = term * weights[k]
            acc = term if acc is None else acc + term
        if consume is not None:
            consume(r0, acc)
        else:
            outs.append(acc)
    if consume is not None:
        return None
    return outs[0] if len(outs) == 1 else jnp.concatenate(outs, axis=0)


def _conv_a(ea_ref, sh_ref, seg, weights, rows, consume):
    ha = CONV_A_WIDTH - 1
    span = rows + PAD_A - SUBLANES
    for r in range(1, SUBLANES):
        sh_ref[r - 1, seg] = ea_ref[seg, pl.ds(r, span), :]
    rb = min(ROW_BLOCK, rows)
    for r0 in range(0, rows, rb):
        acc = None
        for k in range(CONV_A_WIDTH):
            first = PAD_A - ha + k
            r = first % SUBLANES
            if r == 0:
                term = ea_ref[seg, pl.ds(first + r0, rb), :]
            else:
                term = sh_ref[r - 1, seg, pl.ds(first - r + r0, rb), :]
            term = term * weights[k]
            acc = term if acc is None else acc + term
        consume(r0, acc)


def _mix_kernel(x_ref, g_ref, w_ref, caw_ref, cab_ref, lnag_ref, lnab_ref, cbw_ref, lncg_ref, lncb_ref,
                wmix_ref, bmix_ref, poolw_ref, pscale_ref, ha_ref, hb_ref, hd_ref,
                h_ref, acts_ref, na_ref, nb_ref, nd_ref, vr_ref,
                ea_ref, eb_ref, ed_ref, sh_ref, pw_ref, *, nseg, seg_len, start, mix_chunk, vrows):
    i = pl.program_id(1)
    L = seg_len
    rows = nseg * L
    ha, hb, hd = CONV_A_WIDTH - 1, CONV_B_WIDTH - 1, POOL_HIST

    @pl.when(i == 0)
    def _load_history():
        ea_ref[:, 0:PAD_A - ha, :] = jnp.zeros((nseg, PAD_A - ha, D_BR), F32)
        ea_ref[:, PAD_A - ha:PAD_A, :] = ha_ref[...]
        eb_ref[:, PAD_B - hb:PAD_B, :] = hb_ref[...]
        ed_ref[:, 0:PAD_D - hd, :] = jnp.zeros((nseg, PAD_D - hd, D_BR), F32)
        ed_ref[:, PAD_D - hd:PAD_D, :] = hd_ref[...]
        pw_ref[:, :, 0:SUBLANES, :] = jnp.zeros((2, nseg, SUBLANES, D_BR), F32)

    caw = [caw_ref[k:k + 1, :] for k in range(CONV_A_WIDTH)]
    cbw = [cbw_ref[k:k + 1, :] for k in range(CONV_B_WIDTH)]
    segs = [(s, s * L) for s in range(nseg)]

    h = _rmsnorm(x_ref[...], g_ref[...]).astype(BF16)
    h_ref[...] = h

    za = _dot(h, w_ref[:, COL_A[0]:COL_A[1]])
    cb = min(CHUNK_ROWS, L)
    for s, q0 in segs:
        for c0 in range(0, L, cb):
            zc_ = za[q0 + c0:q0 + c0 + cb]
            ea_ref[s, PAD_A + c0:PAD_A + c0 + cb, :] = zc_[:, :D_BR] * _sigmoid(zc_[:, D_BR:])

        def finish_a(r0, conv, q0=q0):
            a_act = _silu(_layernorm(conv + cab_ref[...], lnag_ref[...], lnab_ref[...]))
            acts_ref[q0 + r0:q0 + r0 + conv.shape[0], 0:D_BR] = a_act.astype(BF16)

        _conv_a(ea_ref, sh_ref, s, caw, L, finish_a)

    zb = _dot(h, w_ref[:, COL_B[0]:COL_B[1]])
    for s, q0 in segs:
        for c0 in range(0, L, cb):
            zc_ = zb[q0 + c0:q0 + c0 + cb]
            eb_ref[s, PAD_B + c0:PAD_B + c0 + cb, :] = zc_[:, D_BR:2 * D_BR] * zc_[:, 2 * D_BR:]

        def finish_b(r0, conv, q0=q0):
            n = conv.shape[0]
            acts_ref[q0 + r0:q0 + r0 + n, D_BR:2 * D_BR] = (zb[q0 + r0:q0 + r0 + n, :D_BR] * conv).astype(BF16)

        _windowed_sum(eb_ref, s, PAD_B - hb, L, CONV_B_WIDTH, cbw, consume=finish_b)

    zc = _dot(h, w_ref[:, COL_C[0]:COL_C[1]])
    for c0 in range(0, rows, mix_chunk):
        v = _layernorm(_gelu_erf(zc[c0:c0 + mix_chunk, D_BR:]), lncg_ref[...], lncb_ref[...])
        for s, q0 in segs:
            lo = q0 + L - vrows
            if c0 <= lo and lo + vrows <= c0 + mix_chunk:
                vr_ref[s] = v[lo - c0:lo - c0 + vrows]
        vb = v.astype(BF16)
        for g in range(C_GROUPS):
            cols = slice(g * C_GROUP_W, (g + 1) * C_GROUP_W)
            mixed = _dot(wmix_ref[g], vb[:, cols]) + bmix_ref[:, cols]
            u = _gelu_erf(zc[c0:c0 + mix_chunk, g * C_GROUP_W:(g + 1) * C_GROUP_W])
            acts_ref[c0:c0 + mix_chunk, 2 * D_BR + g * C_GROUP_W:2 * D_BR + (g + 1) * C_GROUP_W] = (
                u * mixed).astype(BF16)

    zd = _dot(h, w_ref[:, COL_D[0]:COL_D[1]])
    for s, q0 in segs:
        ed_ref[s, PAD_D:PAD_D + L, :] = zd[q0:q0 + L]
        pos = start + i * L + lax.broadcasted_iota(jnp.int32, (L, POOL_GROUP_W), 0)
        span = PAD_D + L - SUBLANES
        g1, g2, g3 = (slice(g * POOL_GROUP_W, D_BR) for g in (1, 2, 3))
        s2 = _windowed_sum(ed_ref, s, SUBLANES - 1, span, 2, rb=PAD_D - SUBLANES)
        pw_ref[0, s, SUBLANES:, g1] = s2[:, g1]
        s4 = pw_ref[0, s, SUBLANES:, g1] + pw_ref[0, s, pl.ds(SUBLANES - 2, span), g1]
        pw_ref[1, s, SUBLANES:, g2] = s4[:, POOL_GROUP_W:]
        s8 = pw_ref[1, s, SUBLANES:, g2] + pw_ref[1, s, pl.ds(SUBLANES - 4, span), g2]
        pw_ref[0, s, SUBLANES:, g3] = s8[:, POOL_GROUP_W:]
        s16 = pw_ref[0, s, PAD_D:, g3] + pw_ref[0, s, PAD_D - SUBLANES:PAD_D - SUBLANES + L, g3]
        t0 = PAD_D - SUBLANES
        sums = (s2[t0:, :POOL_GROUP_W], s4[t0:, :POOL_GROUP_W], s8[t0:, :POOL_GROUP_W], s16)
        for g, wdw in enumerate(POOL_WINDOWS):
            cols = slice(g * POOL_GROUP_W, (g + 1) * POOL_GROUP_W)
            ssum = sums[g]
            cnt = jnp.minimum(pos + 1, wdw).astype(F32)
            pooled = ssum / cnt - zd[q0:q0 + L, cols]
            d_mix = _dot(pooled.astype(BF16), poolw_ref[g])
            acts_ref[q0:q0 + L, 3 * D_BR + g * POOL_GROUP_W:3 * D_BR + (g + 1) * POOL_GROUP_W] = (
                d_mix * pscale_ref[:, cols]).astype(BF16)

    for s in range(nseg):
        for ext_ref, new_ref, pad, hist in ((ea_ref, na_ref, PAD_A, ha), (eb_ref, nb_ref, PAD_B, hb),
                                            (ed_ref, nd_ref, PAD_D, hd)):
            tail = ext_ref[s, pad + L - hist:pad + L, :]
            new_ref[s] = tail
            ext_ref[s, pad - hist:pad, :] = tail


def _mix_call(x, hist_a, hist_b, hist_d, wts, wmix, bmix, *, layer, nseg, seg_len, start):
    nb, t, _ = x.shape
    tm = nseg * seg_len
    n_tiles = t // tm
    assert t % tm == 0 and (nseg == 1 or n_tiles == 1)
    mix_chunk = wmix.shape[-1]
    assert tm % mix_chunk == 0
    vrows = min(seg_len, SPATIAL_CHUNK)

    def of_layer(arr):
        return pl.BlockSpec((None,) + arr.shape[1:], lambda b, i, _n=arr.ndim: (layer,) + (0,) * (_n - 1))

    def per_batch(arr):
        return pl.BlockSpec((None,) + arr.shape[1:], lambda b, i, _n=arr.ndim: (b,) + (0,) * (_n - 1))

    tile = pl.BlockSpec((None, tm, D_MODEL), lambda b, i: (b, i, 0))
    w_br_spec = pl.BlockSpec((None, D_MODEL, BR_COLS), lambda b, i: (layer, 0, 0),
                             pipeline_mode=pl.Buffered(1))
    small = [wts['conv_a_w'], wts['conv_a_b'], wts['ln_a_g'], wts['ln_a_b'], wts['conv_b_w'], wts['ln_c_g'],
             wts['ln_c_b'], wmix, bmix, wts['pool_w'], wts['pool_scale']]
    out_shape = (
        jax.ShapeDtypeStruct((nb, t, D_MODEL), BF16),
        jax.ShapeDtypeStruct((nb, t, D_MODEL), BF16),
        jax.ShapeDtypeStruct((nb, nseg, CONV_A_WIDTH - 1, D_BR), F32),
        jax.ShapeDtypeStruct((nb, nseg, CONV_B_WIDTH - 1, D_BR), F32),
        jax.ShapeDtypeStruct((nb, nseg, POOL_HIST, D_BR), F32),
        jax.ShapeDtypeStruct((nb, nseg, vrows, D_BR), F32),
    )
    kern = functools.partial(_mix_kernel, nseg=nseg, seg_len=seg_len, start=start, mix_chunk=mix_chunk,
                             vrows=vrows)
    return pl.pallas_call(
        kern,
        grid=(nb, n_tiles),
        in_specs=[tile, of_layer(wts['norm_mix_g']), w_br_spec] + [of_layer(a) for a in small]
        + [per_batch(hist_a), per_batch(hist_b), per_batch(hist_d)],
        out_specs=(tile, tile) + tuple(per_batch(s) for s in out_shape[2:]),
        out_shape=out_shape,
        scratch_shapes=[
            pltpu.VMEM((nseg, PAD_A + seg_len, D_BR), F32),
            pltpu.VMEM((nseg, PAD_B + seg_len, D_BR), F32),
            pltpu.VMEM((nseg, PAD_D + seg_len, D_BR), F32),
            pltpu.VMEM((SUBLANES - 1, nseg, PAD_A - SUBLANES + seg_len, D_BR), F32),
            pltpu.VMEM((2, nseg, PAD_D + seg_len, D_BR), F32),
        ],
        compiler_params=pltpu.CompilerParams(dimension_semantics=("arbitrary", "arbitrary"),
                                             vmem_limit_bytes=V7X_VMEM_LIMIT),
        name="mix",
    )(x, wts['norm_mix_g'], wts['w_in'], *small, hist_a, hist_b, hist_d)


def _gate_kernel(h_ref, acts_ref, wg0, wg1, wg2, wg3, wo0, wo1, wo2, wo3, out_ref):
    h = h_ref[...]
    acc = None
    for b, (wg, wo) in enumerate(((wg0, wo0), (wg1, wo1), (wg2, wo2), (wg3, wo3))):
        gate = _sigmoid(_dot(h, wg[...]))
        term = gate * _dot(acts_ref[:, b * D_BR:(b + 1) * D_BR], wo[...])
        acc = term if acc is None else acc + term
    out_ref[...] = acc.astype(BF16)


def _gate_call(h, acts, wts, *, layer, tm, nc):
    n = h.shape[0]
    n_col = D_MODEL // nc
    gate0 = BR_COLS // nc
    act_tile = pl.BlockSpec((tm, D_MODEL), lambda i, j: (i, 0))
    wg_specs = [pl.BlockSpec((None, D_MODEL, nc), lambda i, j, _b=b: (layer, 0, gate0 + _b * n_col + j))
                for b in range(4)]
    wo_spec = pl.BlockSpec((None, D_BR, nc), lambda i, j: (layer, 0, j))
    return pl.pallas_call(
        _gate_kernel,
        grid=(n // tm, n_col),
        in_specs=[act_tile, act_tile] + wg_specs + [wo_spec] * 4,
        out_specs=pl.BlockSpec((tm, nc), lambda i, j: (i, j)),
        out_shape=jax.ShapeDtypeStruct((n, D_MODEL), BF16),
        compiler_params=pltpu.CompilerParams(dimension_semantics=("parallel", "arbitrary"),
                                             vmem_limit_bytes=V7X_VMEM_LIMIT),
        name="gate",
    )(h, acts, wts['w_in'], wts['w_in'], wts['w_in'], wts['w_in'],
      wts['w_out_a'], wts['w_out_b'], wts['w_out_c'], wts['w_out_d'])


def _oproj_kernel(x_ref, m_ref, wo_ref, out_ref):
    out_ref[...] = x_ref[...] + _dot(m_ref[...], wo_ref[...])


def _oproj_call(x, merged, wts, *, layer, tm):
    n = x.shape[0]
    return pl.pallas_call(
        _oproj_kernel,
        grid=(n // tm,),
        in_specs=[pl.BlockSpec((tm, D_MODEL), lambda i: (i, 0)),
                  pl.BlockSpec((tm, D_MODEL), lambda i: (i, 0)),
                  pl.BlockSpec((None, D_MODEL, D_MODEL), lambda i: (layer, 0, 0),
                               pipeline_mode=pl.Buffered(1))],
        out_specs=pl.BlockSpec((tm, D_MODEL), lambda i: (i, 0)),
        out_shape=jax.ShapeDtypeStruct((n, D_MODEL), F32),
        compiler_params=pltpu.CompilerParams(dimension_semantics=("parallel",),
                                             vmem_limit_bytes=V7X_VMEM_LIMIT),
        name="oproj",
    )(x, merged, wts['w_o'])


def _ffn_kernel(x_ref, g_ref, w1_ref, w3_ref, w2_ref, gfin_ref, out_ref, h2_ref, *, final_norm):
    j = pl.program_id(1)

    @pl.when(j == 0)
    def _start():
        x = x_ref[...]
        h2_ref[...] = _rmsnorm(x, g_ref[...]).astype(BF16)
        out_ref[...] = x

    h2 = h2_ref[...]
    hidden = (_silu(_dot(h2, w1_ref[...])) * _dot(h2, w3_ref[...])).astype(BF16)
    out_ref[...] += _dot(hidden, w2_ref[...])

    if final_norm:
        @pl.when(j == pl.num_programs(1) - 1)
        def _finish():
            out_ref[...] = _rmsnorm(out_ref[...], gfin_ref[...])


def _ffn_call(x, wts, norm_final_g, *, layer, tm, fc, final_norm):
    n = x.shape[0]
    kern = functools.partial(_ffn_kernel, final_norm=final_norm)
    return pl.pallas_call(
        kern,
        grid=(n // tm, D_FF // fc),
        in_specs=[pl.BlockSpec((tm, D_MODEL), lambda i, j: (i, 0)),
                  pl.BlockSpec((None, 1, D_MODEL), lambda i, j: (layer, 0, 0)),
                  pl.BlockSpec((None, D_MODEL, fc), lambda i, j: (layer, 0, j)),
                  pl.BlockSpec((None, D_MODEL, fc), lambda i, j: (layer, 0, j)),
                  pl.BlockSpec((None, fc, D_MODEL), lambda i, j: (layer, j, 0)),
                  pl.BlockSpec((1, D_MODEL), lambda i, j: (0, 0))],
        out_specs=pl.BlockSpec((tm, D_MODEL), lambda i, j: (i, 0)),
        out_shape=jax.ShapeDtypeStruct((n, D_MODEL), F32),
        scratch_shapes=[pltpu.VMEM((tm, D_MODEL), BF16)],
        compiler_params=pltpu.CompilerParams(dimension_semantics=("parallel", "arbitrary"),
                                             vmem_limit_bytes=V7X_VMEM_LIMIT),
        name="ffn",
    )(x, wts['norm_ffn_g'], wts['ffn_w1'], wts['ffn_w3'], wts['ffn_w2'], norm_final_g)


def _spatial_mix_operands(spatial_w, spatial_b, seg_len):
    depth = spatial_w.shape[0]
    tri = jnp.tril(jnp.ones((SPATIAL_CHUNK, SPATIAL_CHUNK), dtype=bool))
    wm = jnp.where(tri, spatial_w, jnp.zeros((), spatial_w.dtype))
    bias = jnp.repeat(jnp.swapaxes(spatial_b, 1, 2), C_GROUP_W, axis=2)
    if seg_len >= SPATIAL_CHUNK:
        assert seg_len % SPATIAL_CHUNK == 0
        return wm.astype(BF16), bias
    reps = 2 * SPATIAL_CHUNK // seg_len
    eye = jnp.eye(reps, dtype=wm.dtype)
    corner = wm[:, :, :seg_len, :seg_len]
    blockdiag = jnp.einsum('pq,lgts->lgptqs', eye, corner).reshape(
        depth, C_GROUPS, reps * seg_len, reps * seg_len)
    return blockdiag.astype(BF16), jnp.tile(bias[:, :seg_len], (1, reps, 1))


PROMPT_MIX_TILE = 512
PROMPT_TILE = 1024
GATE_COLS = 512
FFN_COLS = 512


def kernel(x_prompt, x_sample, state_conv_a, state_conv_b, state_pool, norm_mix_g, w_in, conv_a_w, conv_a_b,
           ln_a_g, ln_a_b, w_out_a, conv_b_w, w_out_b, ln_c_g, ln_c_b, spatial_w, spatial_b, w_out_c, pool_w,
           pool_scale, w_out_d, w_o, norm_ffn_g, ffn_w1, ffn_w3, ffn_w2, norm_final_g):
    depth = w_in.shape[0]
    nbp, seq, _ = x_prompt.shape
    nbs, dseq, _ = x_sample.shape

    rows = lambda v: v.reshape(depth, 1, -1)
    wts = dict(
        w_in=w_in.astype(BF16),
        w_out_a=w_out_a.astype(BF16), w_out_b=w_out_b.astype(BF16),
        w_out_c=w_out_c.astype(BF16), w_out_d=w_out_d.astype(BF16),
        w_o=w_o.astype(BF16), pool_w=pool_w.astype(BF16),
        ffn_w1=ffn_w1.astype(BF16), ffn_w3=ffn_w3.astype(BF16), ffn_w2=ffn_w2.astype(BF16),
        norm_mix_g=rows(norm_mix_g), norm_ffn_g=rows(norm_ffn_g),
        conv_a_w=conv_a_w, conv_a_b=rows(conv_a_b), ln_a_g=rows(ln_a_g), ln_a_b=rows(ln_a_b),
        conv_b_w=conv_b_w, ln_c_g=rows(ln_c_g), ln_c_b=rows(ln_c_b), pool_scale=rows(pool_scale),
    )
    gfin = norm_final_g.reshape(1, -1)

    def run_group(x, hist_a, hist_b, hist_d, *, nseg, seg_len, start, tm):
        nb, t, _ = x.shape
        wmix, bmix = _spatial_mix_operands(spatial_w, spatial_b, seg_len)
        states = []
        for l in range(depth):
            h, acts, na, nbuf, nd, vr = _mix_call(x, hist_a[l], hist_b[l], hist_d[l], wts, wmix, bmix,
                                                  layer=l, nseg=nseg, seg_len=seg_len, start=start)
            xf = x.reshape(nb * t, D_MODEL)
            merged = _gate_call(h.reshape(nb * t, D_MODEL), acts.reshape(nb * t, D_MODEL), wts,
                                layer=l, tm=tm, nc=GATE_COLS)
            x1 = _oproj_call(xf, merged, wts, layer=l, tm=tm)
            x = _ffn_call(x1, wts, gfin, layer=l, tm=tm, fc=FFN_COLS,
                          final_norm=(l == depth - 1)).reshape(nb, t, D_MODEL)
            states.append((na, nbuf, nd, vr))
        return (x,) + tuple(jnp.stack(s) for s in zip(*states))

    dt = x_prompt.dtype
    zeros = lambda rows: jnp.zeros((depth, nbp, 1, rows, D_BR), dt)
    yp, ap, bp, dp, vp = run_group(x_prompt, zeros(CONV_A_WIDTH - 1), zeros(CONV_B_WIDTH - 1), zeros(POOL_HIST),
                                   nseg=1, seg_len=PROMPT_MIX_TILE, start=0, tm=PROMPT_TILE)
    ys, a_s, b_s, d_s, v_s = run_group(
        x_sample.reshape(1, nbs * dseq, D_MODEL), state_conv_a[:, None], state_conv_b[:, None],
        state_pool[:, None], nseg=nbs, seg_len=dseq, start=PAST_LEN, tm=nbs * dseq)

    squeeze_p = lambda s: s.reshape(depth, nbp, s.shape[-2], D_BR)
    squeeze_s = lambda s: s.reshape(depth, nbs, s.shape[-2], D_BR)
    return (yp, ys.reshape(nbs, dseq, D_MODEL), squeeze_p(ap), squeeze_s(a_s), squeeze_p(bp), squeeze_s(b_s),
            squeeze_p(dp), squeeze_s(d_s), squeeze_p(vp), squeeze_s(v_s))
```
